```python
import jax, jax.numpy as jnp
from jax import lax
import numpy as np

D_MODEL = 2048
BATCH = 4
SEQ = 4096
DEPTH = 2

N_MIXERS = 2
BLOCK = 128
NEG_INF = -1e30
RMS_EPS = 1e-6
FOX_HEAD_DIM = 64
FOX_HEADS = D_MODEL // FOX_HEAD_DIM
FOX_WIDTH = FOX_HEADS * FOX_HEAD_DIM
FOX_IN = 4 * FOX_WIDTH + FOX_HEADS
SWA_HEAD_DIM = 64
SWA_Q_HEADS = D_MODEL // SWA_HEAD_DIM
SWA_KV_HEADS = SWA_Q_HEADS // 8
SWA_GROUP = SWA_Q_HEADS // SWA_KV_HEADS
SWA_WINDOW = 128
SWA_WIDTH = SWA_Q_HEADS * SWA_HEAD_DIM
SWA_KV_WIDTH = SWA_KV_HEADS * SWA_HEAD_DIM
SWA_IN = 2 * SWA_WIDTH + 2 * SWA_KV_WIDTH
ROPE_THETA = 500000.0
ROT_DIM = SWA_HEAD_DIM // 4
N_FOX_LAYERS = (DEPTH + 1) // 2
N_SWA_LAYERS = DEPTH // 2

kernel_name = "fox_swa_sink_interleaved_gated_hybrid"


def rmsnorm(x, g):
    x32 = x.astype(jnp.float32)
    y = x32 * lax.rsqrt(jnp.mean(x32 * x32, axis=-1, keepdims=True) + RMS_EPS)
    return (y * g.astype(jnp.float32)).astype(x.dtype)


def partial_rope(x, pos):
    half = ROT_DIM // 2
    inv_freq = ROPE_THETA ** (-jnp.arange(half, dtype=jnp.float32) / half)
    ang = pos[:, None] * inv_freq[None, :]
    cos = jnp.cos(ang)[None, :, None, :]
    sin = jnp.sin(ang)[None, :, None, :]
    x32 = x.astype(jnp.float32)
    x1, x2 = x32[..., :half], x32[..., half:ROT_DIM]
    rot = jnp.concatenate([x1 * cos - x2 * sin, x2 * cos + x1 * sin], axis=-1)
    return jnp.concatenate([rot.astype(x.dtype), x[..., ROT_DIM:]], axis=-1)


def fox_attention(q, k, v, log_f):
    B, S, H, d = q.shape
    nb = S // BLOCK
    scale = d ** -0.5
    c = jnp.cumsum(log_f, axis=1).transpose(0, 2, 1)
    kf = k.astype(jnp.float32)
    key_pos = jnp.arange(S)
    qb = q.reshape(B, nb, BLOCK, H, d).transpose(1, 0, 2, 3, 4)
    cb = c.reshape(B, H, nb, BLOCK).transpose(2, 0, 1, 3)

    def one_block(args):
        qi, ci, i = args
        s = jnp.einsum('bqhd,bkhd->bhqk', qi.astype(jnp.float32), kf) * scale
        s = s + ci[..., None] - c[:, :, None, :]
        qpos = i * BLOCK + jnp.arange(BLOCK)
        mask = key_pos[None, :] <= qpos[:, None]
        s = jnp.where(mask[None, None], s, NEG_INF)
        p = jax.nn.softmax(s, axis=-1).astype(v.dtype)
        return jnp.einsum('bhqk,bkhd->bqhd', p, v)

    out = lax.map(one_block, (qb, cb, jnp.arange(nb)))
    return out.transpose(1, 0, 2, 3, 4).reshape(B, S, H * d)


def swa_attention(q, k, v, sinks):
    B, S = q.shape[:2]
    nb = S // BLOCK
    scale = SWA_HEAD_DIM ** -0.5
    qb = q.reshape(B, nb, BLOCK, SWA_KV_HEADS, SWA_GROUP, SWA_HEAD_DIM)

    def band(t):
        tp = jnp.pad(t, ((0, 0), (BLOCK, 0), (0, 0), (0, 0)))
        tp = tp.reshape(B, nb + 1, BLOCK, SWA_KV_HEADS, SWA_HEAD_DIM)
        return jnp.concatenate([tp[:, :-1], tp[:, 1:]], axis=2)

    kw, vw = band(k), band(v)
    s = jnp.einsum('bnqhgd,bnkhd->bnhgqk', qb.astype(jnp.float32), kw.astype(jnp.float32)) * scale
    t_loc = jnp.arange(BLOCK)[:, None]
    j_loc = jnp.arange(2 * BLOCK)[None, :]
    diff = t_loc + BLOCK - j_loc
    key_abs = (jnp.arange(nb)[:, None, None] - 1) * BLOCK + j_loc[None]
    mask = (diff >= 0)[None] & (diff < SWA_WINDOW)[None] & (key_abs >= 0)
    s = jnp.where(mask[None, :, None, None], s, NEG_INF)
    sink = sinks.astype(jnp.float32).reshape(SWA_KV_HEADS, SWA_GROUP)[None, None, :, :, None, None]
    m = jnp.maximum(jnp.max(s, axis=-1, keepdims=True), sink)
    e = jnp.exp(s - m)
    denom = jnp.sum(e, axis=-1, keepdims=True) + jnp.exp(sink - m)
    p = (e / denom).astype(v.dtype)
    o = jnp.einsum('bnhgqk,bnkhd->bnqhgd', p, vw)
    return o.reshape(B, S, SWA_WIDTH)


def setup_inputs(seed: int = 0) -> dict:
    key = jax.random.key(seed)
    ks = jax.random.split(key, 10)
    x = jax.random.normal(ks[0], (BATCH, SEQ, D_MODEL), jnp.float32)
    norm_g = 1.0 + 0.02 * jax.random.normal(ks[1], (DEPTH, D_MODEL), jnp.float32)
    fox_w_in = jax.random.normal(ks[2], (N_FOX_LAYERS, D_MODEL, FOX_IN), jnp.float32) * D_MODEL ** -0.5
    fox_b_f = (jnp.linspace(1.0, 6.0, FOX_HEADS, dtype=jnp.float32)[None, :]
               + 0.1 * jax.random.normal(ks[3], (N_FOX_LAYERS, FOX_HEADS), jnp.float32))
    fox_w_out = jax.random.normal(ks[4], (N_FOX_LAYERS, FOX_WIDTH, D_MODEL), jnp.float32) * FOX_WIDTH ** -0.5
    swa_w_in = jax.random.normal(ks[5], (N_SWA_LAYERS, D_MODEL, SWA_IN), jnp.float32) * D_MODEL ** -0.5
    swa_sinks = 0.5 * jax.random.normal(ks[6], (N_SWA_LAYERS, SWA_Q_HEADS), jnp.float32)
    swa_w_out = jax.random.normal(ks[7], (N_SWA_LAYERS, SWA_WIDTH, D_MODEL), jnp.float32) * SWA_WIDTH ** -0.5
    final_g = 1.0 + 0.02 * jax.random.normal(ks[8], (D_MODEL,), jnp.float32)
    return {"x": x, "norm_g": norm_g, "fox_w_in": fox_w_in, "fox_b_f": fox_b_f,
            "fox_w_out": fox_w_out, "swa_w_in": swa_w_in, "swa_sinks": swa_sinks,
            "swa_w_out": swa_w_out, "final_g": final_g}


def reference(x, norm_g, fox_w_in, fox_b_f, fox_w_out, swa_w_in, swa_sinks, swa_w_out, final_g):
    B, S, _ = x.shape
    pos = jnp.arange(S, dtype=jnp.float32)
    for i in range(DEPTH):
        h = rmsnorm(x, norm_g[i])
        j = i // N_MIXERS
        if i % N_MIXERS == 0:
            p = h @ fox_w_in[j]
            W = FOX_WIDTH
            q = p[..., :W].reshape(B, S, FOX_HEADS, FOX_HEAD_DIM)
            k = p[..., W:2 * W].reshape(B, S, FOX_HEADS, FOX_HEAD_DIM)
            v = p[..., 2 * W:3 * W].reshape(B, S, FOX_HEADS, FOX_HEAD_DIM)
            gate = p[..., 3 * W:4 * W]
            log_f = jax.nn.log_sigmoid(p[..., 4 * W:].astype(jnp.float32) + fox_b_f[j].astype(jnp.float32))
            y = fox_attention(q, k, v, log_f)
            w_out = fox_w_out[j]
        else:
            p = h @ swa_w_in[j]
            WQ, WK = SWA_WIDTH, SWA_KV_WIDTH
            q = p[..., :WQ].reshape(B, S, SWA_Q_HEADS, SWA_HEAD_DIM)
            k = p[..., WQ:WQ + WK].reshape(B, S, SWA_KV_HEADS, SWA_HEAD_DIM)
            v = p[..., WQ + WK:WQ + 2 * WK].reshape(B, S, SWA_KV_HEADS, SWA_HEAD_DIM)
            gate = p[..., WQ + 2 * WK:]
            q = partial_rope(q, pos)
            k = partial_rope(k, pos)
            y = swa_attention(q, k, v, swa_sinks[j])
            w_out = swa_w_out[j]
        y = y * jax.nn.silu(gate)
        x = x + y @ w_out
    return rmsnorm(x, final_g)
```

```python
import functools
import math

import jax
import jax.numpy as jnp
from jax import lax
from jax.experimental import pallas as pl
from jax.experimental.pallas import tpu as pltpu

F32 = jnp.float32
BF16 = jnp.bfloat16

RMS_EPS = 1e-6
NEG_INF = -1e30
HEAD_DIM = 64
FOX_HEADS = 32
SWA_Q_HEADS = 32
SWA_KV_HEADS = 4
SWA_GROUP = SWA_Q_HEADS // SWA_KV_HEADS
SWA_WINDOW = 128
ROPE_THETA = 500000.0
ROT_DIM = HEAD_DIM // 4
LANES = 128
VMEM_CAPACITY = 64 * 1024 * 1024


def _vmem_limit(pipelined_block_bytes, scratch_bytes, temp_bytes):
    need = 2 * pipelined_block_bytes + scratch_bytes + temp_bytes
    return int(min(need, VMEM_CAPACITY - (4 << 20)))


def _norm_proj_kernel(x_ref, g_ref, w_ref, wf_ref, o_ref, of_ref, h_ref):
    @pl.when(pl.program_id(1) == 0)
    def _():
        x = x_ref[...]
        ms = jnp.mean(x * x, axis=-1, keepdims=True)
        h = (x * lax.rsqrt(ms + RMS_EPS)) * g_ref[...]
        h_ref[...] = h.astype(BF16)
        of_ref[...] = jnp.dot(h_ref[...], wf_ref[...], preferred_element_type=F32)

    o_ref[...] = jnp.dot(h_ref[...], w_ref[...], preferred_element_type=F32).astype(o_ref.dtype)


def _norm_proj(x2d, g, w, wf, *, tm, tn):
    m, d = x2d.shape
    n = w.shape[1]
    blocks = tm * d * 4 + d * 4 + d * tn * 2 + d * LANES * 2 + tm * tn * 2 + tm * LANES * 4
    return pl.pallas_call(
        _norm_proj_kernel,
        grid=(m // tm, n // tn),
        in_specs=[
            pl.BlockSpec((tm, d), lambda i, j: (i, 0)),
            pl.BlockSpec((1, d), lambda i, j: (0, 0)),
            pl.BlockSpec((d, tn), lambda i, j: (0, j)),
            pl.BlockSpec((d, LANES), lambda i, j: (0, 0)),
        ],
        out_specs=[
            pl.BlockSpec((tm, tn), lambda i, j: (i, j)),
            pl.BlockSpec((tm, LANES), lambda i, j: (i, 0)),
        ],
        out_shape=[
            jax.ShapeDtypeStruct((m, n), BF16),
            jax.ShapeDtypeStruct((m, LANES), F32),
        ],
        scratch_shapes=[pltpu.VMEM((tm, d), BF16)],
        compiler_params=pltpu.CompilerParams(
            dimension_semantics=("parallel", "arbitrary"),
            vmem_limit_bytes=_vmem_limit(blocks, tm * d * 2, 3 * tm * d * 4),
        ),
        name="norm_proj",
    )(x2d, g, w, wf)


def _log_decay_cumsum_kernel(z_ref, b_ref, c_ref):
    z = z_ref[0] + b_ref[...]
    x = jnp.minimum(z, 0.0) - jnp.log1p(jnp.exp(-jnp.abs(z)))
    s = x.shape[0]
    row = lax.broadcasted_iota(jnp.int32, x.shape, 0)
    shift = 1
    while shift < s:
        x = x + jnp.where(row >= shift, pltpu.roll(x, shift, axis=0), 0.0)
        shift *= 2
    c_ref[0] = x


def _log_decay_cumsum(z, b):
    bsz, s, _ = z.shape
    blk = s * LANES * 4
    return pl.pallas_call(
        _log_decay_cumsum_kernel,
        grid=(bsz,),
        in_specs=[
            pl.BlockSpec((1, s, LANES), lambda i: (i, 0, 0)),
            pl.BlockSpec((1, LANES), lambda i: (0, 0)),
        ],
        out_specs=pl.BlockSpec((1, s, LANES), lambda i: (i, 0, 0)),
        out_shape=jax.ShapeDtypeStruct(z.shape, F32),
        compiler_params=pltpu.CompilerParams(
            dimension_semantics=("parallel",),
            vmem_limit_bytes=_vmem_limit(2 * blk, 0, 6 * blk),
        ),
        name="log_decay_cumsum",
    )(z, b)


def _fox_attn_kernel(q_ref, k_ref, v_ref, cq_ref, ck_ref, o_ref, m_ref, l_ref, acc_ref, *, tq, tk):
    qi = pl.program_id(2)
    ki = pl.program_id(3)
    last_k = ((qi + 1) * tq - 1) // tk

    @pl.when(ki == 0)
    def _():
        m_ref[...] = jnp.full(m_ref.shape, NEG_INF, F32)
        l_ref[...] = jnp.zeros(l_ref.shape, F32)
        acc_ref[...] = jnp.zeros(acc_ref.shape, F32)

    def step(masked):
        s = lax.dot_general(q_ref[0, 0], k_ref[0, 0], (((1,), (1,)), ((), ())),
                            preferred_element_type=F32)
        s = s * (HEAD_DIM ** -0.5) + cq_ref[0, 0] - ck_ref[0, 0]
        if masked:
            qpos = qi * tq + lax.broadcasted_iota(jnp.int32, (tq, tk), 0)
            kpos = ki * tk + lax.broadcasted_iota(jnp.int32, (tq, tk), 1)
            s = jnp.where(kpos <= qpos, s, NEG_INF)
        m_prev = m_ref[...]
        m_new = jnp.maximum(m_prev, jnp.max(s, axis=-1, keepdims=True))
        alpha = jnp.exp(m_prev - m_new)
        p = jnp.exp(s - m_new)
        l_ref[...] = alpha * l_ref[...] + jnp.sum(p, axis=-1, keepdims=True)
        acc_ref[...] = alpha * acc_ref[...] + jnp.dot(p.astype(BF16), v_ref[0, 0],
                                                      preferred_element_type=F32)
        m_ref[...] = m_new

    @pl.when((ki + 1) * tk - 1 <= qi * tq)
    def _():
        step(False)

    @pl.when(((ki + 1) * tk - 1 > qi * tq) & (ki <= last_k))
    def _():
        step(True)

    @pl.when(ki == last_k)
    def _():
        o_ref[0, 0] = (acc_ref[...] / l_ref[...]).astype(o_ref.dtype)


def _fox_attention(q, k, v, cq, ck, *, tq, tk):
    bsz, h, s, d = q.shape

    def kv_map(b, hh, qi, ki):
        return (b, hh, jnp.minimum(ki, ((qi + 1) * tq - 1) // tk), 0)

    def ck_map(b, hh, qi, ki):
        return (b, hh, 0, jnp.minimum(ki, ((qi + 1) * tq - 1) // tk))

    blocks = 2 * tq * LANES * 2 + 2 * tk * LANES * 2 + tq * LANES * 4 + 8 * tk * 4
    return pl.pallas_call(
        functools.partial(_fox_attn_kernel, tq=tq, tk=tk),
        grid=(bsz, h, s // tq, s // tk),
        in_specs=[
            pl.BlockSpec((1, 1, tq, d), lambda b, hh, qi, ki: (b, hh, qi, 0)),
            pl.BlockSpec((1, 1, tk, d), kv_map),
            pl.BlockSpec((1, 1, tk, d), kv_map),
            pl.BlockSpec((1, 1, tq, 1), lambda b, hh, qi, ki: (b, hh, qi, 0)),
            pl.BlockSpec((1, 1, 1, tk), ck_map),
        ],
        out_specs=pl.BlockSpec((1, 1, tq, d), lambda b, hh, qi, ki: (b, hh, qi, 0)),
        out_shape=jax.ShapeDtypeStruct((bsz, h, s, d), BF16),
        scratch_shapes=[
            pltpu.VMEM((tq, 1), F32),
            pltpu.VMEM((tq, 1), F32),
            pltpu.VMEM((tq, d), F32),
        ],
        compiler_params=pltpu.CompilerParams(
            dimension_semantics=("parallel", "parallel", "parallel", "arbitrary"),
            vmem_limit_bytes=_vmem_limit(blocks, 3 * tq * LANES * 4, 6 * tq * tk * 4),
        ),
        name="fox_attention",
    )(q, k, v, cq, ck)


def _rope(x, cos, sgn_sin, swap):
    xs = jnp.dot(x, swap, preferred_element_type=F32)
    return (x.astype(F32) * cos + xs * sgn_sin).astype(BF16)


def _swa_attn_kernel(q_ref, kp_ref, kc_ref, vp_ref, vc_ref, cosq_ref, sinq_ref, cosp_ref, sinp_ref,
                     swap_ref, sink_ref, o_ref, *, blk):
    n = pl.program_id(2)
    g = q_ref.shape[1]
    swap = swap_ref[...]
    cosq = cosq_ref[...]
    sinq = sinq_ref[...]
    q = q_ref[0].reshape(g * blk, HEAD_DIM)
    q = _rope(q, jnp.tile(cosq, (g, 1)), jnp.tile(sinq, (g, 1)), swap)
    kc = _rope(kc_ref[0, 0], cosq, sinq, swap)
    kp = _rope(kp_ref[0, 0], cosp_ref[...], sinp_ref[...], swap)
    k = jnp.concatenate([kp, kc], axis=0)
    v = jnp.concatenate([vp_ref[0, 0], vc_ref[0, 0]], axis=0)

    s = lax.dot_general(q, k, (((1,), (1,)), ((), ())), preferred_element_type=F32)
    s = s * (HEAD_DIM ** -0.5)
    t_loc = lax.broadcasted_iota(jnp.int32, (g * blk, 2 * blk), 0) & (blk - 1)
    j_loc = lax.broadcasted_iota(jnp.int32, (g * blk, 2 * blk), 1)
    diff = t_loc + blk - j_loc
    mask = (diff >= 0) & (diff < SWA_WINDOW) & ((n - 1) * blk + j_loc >= 0)
    s = jnp.where(mask, s, NEG_INF)
    sink = sink_ref[0]
    m = jnp.maximum(jnp.max(s, axis=-1, keepdims=True), sink)
    e = jnp.exp(s - m)
    denom = jnp.sum(e, axis=-1, keepdims=True) + jnp.exp(sink - m)
    p = (e / denom).astype(BF16)
    o = jnp.dot(p, v, preferred_element_type=F32)
    o_ref[0] = o.reshape(g, blk, HEAD_DIM).astype(o_ref.dtype)


def _swa_attention(q, k, v, cos, sgn_sin, swap, sink_rows, *, blk):
    bsz, hq, s, d = q.shape
    hkv = k.shape[1]
    g = hq // hkv

    def cur(b, h, n):
        return (b, h, n, 0)

    def prev(b, h, n):
        return (b, h, jnp.maximum(n - 1, 0), 0)

    blocks = (g * blk + 4 * blk) * LANES * 2 + 4 * blk * LANES * 4 + g * blk * LANES * 4 \
        + g * blk * LANES * 2
    return pl.pallas_call(
        functools.partial(_swa_attn_kernel, blk=blk),
        grid=(bsz, hkv, s // blk),
        in_specs=[
            pl.BlockSpec((1, g, blk, d), lambda b, h, n: (b, h, n, 0)),
            pl.BlockSpec((1, 1, blk, d), prev),
            pl.BlockSpec((1, 1, blk, d), cur),
            pl.BlockSpec((1, 1, blk, d), prev),
            pl.BlockSpec((1, 1, blk, d), cur),
            pl.BlockSpec((blk, d), lambda b, h, n: (n, 0)),
            pl.BlockSpec((blk, d), lambda b, h, n: (n, 0)),
            pl.BlockSpec((blk, d), lambda b, h, n: (jnp.maximum(n - 1, 0), 0)),
            pl.BlockSpec((blk, d), lambda b, h, n: (jnp.maximum(n - 1, 0), 0)),
            pl.BlockSpec((d, d), lambda b, h, n: (0, 0)),
            pl.BlockSpec((1, g * blk, 1), lambda b, h, n: (h, 0, 0)),
        ],
        out_specs=pl.BlockSpec((1, g, blk, d), lambda b, h, n: (b, h, n, 0)),
        out_shape=jax.ShapeDtypeStruct((bsz, hq, s, d), BF16),
        compiler_params=pltpu.CompilerParams(
            dimension_semantics=("parallel", "parallel", "parallel"),
            vmem_limit_bytes=_vmem_limit(blocks, 0, 8 * g * blk * 2 * blk * 4),
        ),
        name="swa_attention",
    )(q, k, k, v, v, cos, sgn_sin, cos, sgn_sin, swap, sink_rows)


def _gate_out_proj_kernel(y_ref, gate_ref, x_ref, w_ref, fg_ref, o_ref, *, final_norm):
    gate = gate_ref[...].astype(F32)
    z = y_ref[...].astype(F32) * (gate * jax.nn.sigmoid(gate))
    out = x_ref[...] + jnp.dot(z.astype(BF16), w_ref[...], preferred_element_type=F32)
    if final_norm:
        ms = jnp.mean(out * out, axis=-1, keepdims=True)
        out = (out * lax.rsqrt(ms + RMS_EPS)) * fg_ref[...]
    o_ref[...] = out


def _gate_out_proj(y, p, gate_col_block, x2d, w, fg, *, tm, final_norm):
    m, wid = y.shape
    d = w.shape[1]
    blocks = 2 * tm * wid * 2 + 2 * tm * d * 4 + wid * d * 2 + d * 4
    return pl.pallas_call(
        functools.partial(_gate_out_proj_kernel, final_norm=final_norm),
        grid=(m // tm,),
        in_specs=[
            pl.BlockSpec((tm, wid), lambda i: (i, 0)),
            pl.BlockSpec((tm, wid), lambda i: (i, gate_col_block)),
            pl.BlockSpec((tm, d), lambda i: (i, 0)),
            pl.BlockSpec((wid, d), lambda i: (0, 0)),
            pl.BlockSpec((1, d), lambda i: (0, 0)),
        ],
        out_specs=pl.BlockSpec((tm, d), lambda i: (i, 0)),
        out_shape=jax.ShapeDtypeStruct((m, d), F32),
        compiler_params=pltpu.CompilerParams(
            dimension_semantics=("parallel",),
            vmem_limit_bytes=_vmem_limit(blocks, 0, 4 * tm * wid * 4),
        ),
        name="gate_out_proj",
    )(y, p, x2d, w, fg)


def _rope_tables(s):
    half = ROT_DIM // 2
    inv_freq = ROPE_THETA ** (-jnp.arange(half, dtype=F32) / half)
    ang = jnp.arange(s, dtype=F32)[:, None] * inv_freq[None, :]
    cos, sin = jnp.cos(ang), jnp.sin(ang)
    ones = jnp.ones((s, HEAD_DIM - ROT_DIM), F32)
    zeros = jnp.zeros((s, HEAD_DIM - ROT_DIM), F32)
    cos_t = jnp.concatenate([cos, cos, ones], axis=-1)
    sgn_sin_t = jnp.concatenate([-sin, sin, zeros], axis=-1)
    idx = jnp.arange(HEAD_DIM)
    partner = jnp.where(idx < half, idx + half, jnp.where(idx < ROT_DIM, idx - half, idx))
    swap = (idx[:, None] == partner[None, :]).astype(BF16)
    return cos_t, sgn_sin_t, swap


def _heads(p2d, bsz, s, col0, n_heads):
    x = p2d[:, col0:col0 + n_heads * HEAD_DIM].reshape(bsz, s, n_heads, HEAD_DIM)
    return x.transpose(0, 2, 1, 3)


def kernel(x, norm_g, fox_w_in, fox_b_f, fox_w_out, swa_w_in, swa_sinks, swa_w_out, final_g):
    bsz, s, d = x.shape
    m = bsz * s
    x2d = x.reshape(m, d)
    zeros_wf = jnp.zeros((d, LANES), BF16)

    wid = FOX_HEADS * HEAD_DIM
    w_in = fox_w_in[0]
    w_main = w_in[:, :4 * wid].astype(BF16)
    w_f = jnp.pad(w_in[:, 4 * wid:], ((0, 0), (0, LANES - FOX_HEADS))).astype(BF16)
    p, z = _norm_proj(x2d, norm_g[0][None, :], w_main, w_f, tm=512, tn=1024)
    b_f = jnp.pad(fox_b_f[0], (0, LANES - FOX_HEADS))[None, :]
    c = _log_decay_cumsum(z.reshape(bsz, s, LANES), b_f)[:, :, :FOX_HEADS]
    c_t = c.transpose(0, 2, 1)
    q = _heads(p, bsz, s, 0, FOX_HEADS)
    k = _heads(p, bsz, s, wid, FOX_HEADS)
    v = _heads(p, bsz, s, 2 * wid, FOX_HEADS)
    y = _fox_attention(q, k, v, c_t[:, :, :, None], c_t[:, :, None, :], tq=512, tk=512)
    y2d = y.transpose(0, 2, 1, 3).reshape(m, wid)
    x2d = _gate_out_proj(y2d, p, 3, x2d, fox_w_out[0].astype(BF16), final_g[None, :],
                         tm=256, final_norm=False)

    wq = SWA_Q_HEADS * HEAD_DIM
    wk = SWA_KV_HEADS * HEAD_DIM
    p, _ = _norm_proj(x2d, norm_g[1][None, :], swa_w_in[0].astype(BF16), zeros_wf, tm=512, tn=512)
    q = _heads(p, bsz, s, 0, SWA_Q_HEADS)
    k = _heads(p, bsz, s, wq, SWA_KV_HEADS)
    v = _heads(p, bsz, s, wq + wk, SWA_KV_HEADS)
    gate = p[:, wq + 2 * wk:]
    cos_t, sgn_sin_t, swap = _rope_tables(s)
    sink_rows = jnp.repeat(swa_sinks[0].reshape(SWA_KV_HEADS, SWA_GROUP), SWA_WINDOW, axis=1)[..., None]
    y = _swa_attention(q, k, v, cos_t, sgn_sin_t, swap, sink_rows, blk=SWA_WINDOW)
    y2d = y.transpose(0, 2, 1, 3).reshape(m, wq)
    out = _gate_out_proj(y2d, gate, 0, x2d, swa_w_out[0].astype(BF16), final_g[None, :],
                         tm=256, final_norm=True)
    return out.reshape(bsz, s, d)
```

```python
import functools
import math

import jax
import jax.numpy as jnp
from jax import lax
from jax.experimental import pallas as pl
from jax.experimental.pallas import tpu as pltpu

F32 = jnp.float32
BF16 = jnp.bfloat16

RMS_EPS = 1e-6
NEG_INF = -1e30
LOG2E = 1.4426950408889634
HEAD_DIM = 64
FOX_HEADS = 32
SWA_Q_HEADS = 32
SWA_KV_HEADS = 4
SWA_GROUP = SWA_Q_HEADS // SWA_KV_HEADS
SWA_WINDOW = 128
ROPE_THETA = 500000.0
ROT_DIM = HEAD_DIM // 4
ROT_HALF = ROT_DIM // 2
XROWS = 16
LANES = 128
VMEM_CAPACITY = 64 * 1024 * 1024
QK_SCALE = HEAD_DIM ** -0.5 * LOG2E


def _vmem_limit(pipelined_block_bytes, scratch_bytes, temp_bytes):
    need = 2 * pipelined_block_bytes + scratch_bytes + temp_bytes
    return int(min(need, VMEM_CAPACITY - (4 << 20)))


def _rope_rows(res, cos, sin, base):
    x1 = res[base:base + ROT_HALF]
    x2 = res[base + ROT_HALF:base + ROT_DIM]
    return jnp.concatenate([x1 * cos - x2 * sin, x2 * cos + x1 * sin], axis=0)


_NT = (((1,), (1,)), ((), ()))


def _norm_to_scratch(x_ref, g_ref, h_ref):
    x = x_ref[...]
    ms = jnp.mean(x * x, axis=-1, keepdims=True)
    h_ref[...] = ((x * lax.rsqrt(ms + RMS_EPS)) * g_ref[...]).astype(BF16)


def _norm_proj_fox_kernel(x_ref, g_ref, wt_ref, wft_ref, o_ref, of_ref, h_ref, *, q_blocks):
    j = pl.program_id(1)

    @pl.when(j == 0)
    def _():
        _norm_to_scratch(x_ref, g_ref, h_ref)
        of_ref[0] = lax.dot_general(wft_ref[...], h_ref[...], _NT, preferred_element_type=F32)

    res = lax.dot_general(wt_ref[...], h_ref[...], _NT, preferred_element_type=F32)
    o_ref[0] = (res * jnp.where(j < q_blocks, QK_SCALE, 1.0)).astype(o_ref.dtype)


def _norm_proj_swa_kernel(x_ref, g_ref, wt_ref, cos_ref, sin_ref, o_ref, h_ref, *, q_blocks):
    j = pl.program_id(1)

    @pl.when(j == 0)
    def _():
        _norm_to_scratch(x_ref, g_ref, h_ref)

    res = lax.dot_general(wt_ref[...], h_ref[...], _NT, preferred_element_type=F32)
    heads = res.shape[0] // HEAD_DIM

    def write(n_rope, scale):
        cos, sin = cos_ref[...], sin_ref[...]
        for hh in range(heads):
            base = hh * HEAD_DIM
            if hh < n_rope:
                o_ref[0, base:base + ROT_DIM, :] = (_rope_rows(res, cos, sin, base) * scale).astype(o_ref.dtype)
                o_ref[0, base + ROT_DIM:base + HEAD_DIM, :] = (
                    res[base + ROT_DIM:base + HEAD_DIM] * scale).astype(o_ref.dtype)
            else:
                o_ref[0, base:base + HEAD_DIM, :] = res[base:base + HEAD_DIM].astype(o_ref.dtype)

    @pl.when(j < q_blocks)
    def _():
        write(heads, QK_SCALE)

    @pl.when(j == q_blocks)
    def _():
        write(SWA_KV_HEADS, 1.0)

    @pl.when(j > q_blocks)
    def _():
        o_ref[0] = res.astype(o_ref.dtype)


def _norm_proj(x2d, g, wt, extra, *, bsz, mode, tm, tn, q_blocks):
    m, d = x2d.shape
    n = wt.shape[0]
    s = m // bsz
    sb = s // tm
    in_specs = [
        pl.BlockSpec((tm, d), lambda i, j: (i, 0)),
        pl.BlockSpec((1, d), lambda i, j: (0, 0)),
        pl.BlockSpec((tn, d), lambda i, j: (j, 0)),
    ]
    out_specs = [pl.BlockSpec((1, tn, tm), lambda i, j: (i // sb, j, i % sb))]
    out_shape = [jax.ShapeDtypeStruct((bsz, n, s), BF16)]
    blocks = tm * d * 4 + d * 4 + tn * d * 2 + tn * tm * 2
    if mode == "fox":
        body = _norm_proj_fox_kernel
        in_specs.append(pl.BlockSpec((LANES, d), lambda i, j: (0, 0)))
        out_specs.append(pl.BlockSpec((1, LANES, tm), lambda i, j: (i // sb, 0, i % sb)))
        out_shape.append(jax.ShapeDtypeStruct((bsz, LANES, s), F32))
        blocks += LANES * d * 2 + LANES * tm * 4
    else:
        body = _norm_proj_swa_kernel
        in_specs += [pl.BlockSpec((ROT_HALF, tm), lambda i, j: (0, i % sb))] * 2
        blocks += 2 * ROT_HALF * tm * 4
    return pl.pallas_call(
        functools.partial(body, q_blocks=q_blocks),
        grid=(m // tm, n // tn),
        in_specs=in_specs,
        out_specs=out_specs,
        out_shape=out_shape,
        scratch_shapes=[pltpu.VMEM((tm, d), BF16)],
        compiler_params=pltpu.CompilerParams(
            dimension_semantics=("parallel", "arbitrary"),
            vmem_limit_bytes=_vmem_limit(blocks, tm * d * 2, 3 * tm * d * 4 + 2 * tn * tm * 4),
        ),
        name="norm_proj_" + mode,
    )(x2d, g, wt, *extra)


def _decay_rows_kernel(z_ref, b_ref, qx_ref, kx_ref):
    z = z_ref[0] + b_ref[...]
    x = (jnp.minimum(z, 0.0) - jnp.log1p(jnp.exp(-jnp.abs(z)))) * LOG2E
    n_heads, s = x.shape
    col = lax.broadcasted_iota(jnp.int32, x.shape, 1)
    shift = 1
    while shift < s:
        x = x + jnp.where(col >= shift, pltpu.roll(x, shift, axis=1), 0.0)
        shift *= 2
    hi = x.astype(BF16).astype(F32)
    mid = (x - hi).astype(BF16).astype(F32)
    lo = (x - hi - mid).astype(BF16).astype(F32)
    row = lax.broadcasted_iota(jnp.int32, (XROWS, s), 0)
    for h in range(n_heads):
        parts_q = jnp.where(row == 0, hi[h:h + 1], jnp.where(row == 1, mid[h:h + 1], lo[h:h + 1]))
        parts_k = jnp.where(row == 3, hi[h:h + 1], jnp.where(row == 4, mid[h:h + 1], lo[h:h + 1]))
        qx = jnp.where(row < 3, parts_q, jnp.where(row < 6, 1.0, 0.0))
        kx = jnp.where(row < 3, 1.0, jnp.where(row < 6, -parts_k, 0.0))
        qx_ref[0, h] = qx.astype(BF16)
        kx_ref[0, h] = kx.astype(BF16)


def _decay_rows(z, b, n_heads):
    bsz, _, s = z.shape
    out_blk = n_heads * XROWS * s * 2
    return pl.pallas_call(
        _decay_rows_kernel,
        grid=(bsz,),
        in_specs=[
            pl.BlockSpec((1, n_heads, s), lambda i: (i, 0, 0)),
            pl.BlockSpec((n_heads, 1), lambda i: (0, 0)),
        ],
        out_specs=[pl.BlockSpec((1, n_heads, XROWS, s), lambda i: (i, 0, 0, 0))] * 2,
        out_shape=[jax.ShapeDtypeStruct((bsz, n_heads, XROWS, s), BF16)] * 2,
        compiler_params=pltpu.CompilerParams(
            dimension_semantics=("parallel",),
            vmem_limit_bytes=_vmem_limit(n_heads * s * 4 + 2 * out_blk, 0, 8 * n_heads * s * 4),
        ),
        name="decay_rows",
    )(z, b)


def _fox_attn_kernel(q_ref, k_ref, v_ref, gate_ref, qx_ref, kx_ref, o_ref,
                     qa_ref, s0_ref, s1_ref, m_ref, acc_ref, *, g_heads, tq, tk):
    qi = pl.program_id(2)
    n_diag = tq // tk
    n_full = qi * n_diag
    d = HEAD_DIM
    s_refs = (s0_ref, s1_ref)

    for g in range(g_heads):
        qa_ref[g, 0:d, :] = q_ref[0, g * d:(g + 1) * d, :]
        qa_ref[g, d:d + XROWS, :] = qx_ref[0, g]
    m_ref[...] = jnp.full(m_ref.shape, NEG_INF, F32)
    acc_ref[...] = jnp.zeros(acc_ref.shape, F32)

    ones_rows = (lax.broadcasted_iota(jnp.int32, (XROWS, tk), 0) == 0).astype(BF16)

    def scores(j, slot):
        off = pl.multiple_of(j * tk, tk)
        for g in range(g_heads):
            ka = jnp.concatenate([k_ref[0, g * d:(g + 1) * d, pl.ds(off, tk)],
                                  kx_ref[0, g, :, pl.ds(off, tk)]], axis=0)
            s_refs[slot][g] = lax.dot_general(ka, qa_ref[g], (((0,), (0,)), ((), ())),
                                              preferred_element_type=F32)

    def softmax_pv(j, slot, diag):
        off = pl.multiple_of(j * tk, tk)
        for g in range(g_heads):
            s = s_refs[slot][g]
            if diag is not None:
                kpos = diag * tk + lax.broadcasted_iota(jnp.int32, (tk, tq), 0)
                qpos = lax.broadcasted_iota(jnp.int32, (tk, tq), 1)
                s = jnp.where(kpos <= qpos, s, NEG_INF)
            m_prev = m_ref[g]
            m_new = jnp.maximum(m_prev, jnp.max(s, axis=0, keepdims=True))
            alpha = jnp.exp2(m_prev - m_new)
            p = jnp.exp2(s - m_new).astype(BF16)
            va = jnp.concatenate([v_ref[0, g * d:(g + 1) * d, pl.ds(off, tk)], ones_rows], axis=0)
            acc_ref[g] = alpha * acc_ref[g] + jnp.dot(va, p, preferred_element_type=F32)
            m_ref[g] = m_new

    scores(0, 0)

    def two_steps(jj, carry):
        j = 2 * jj
        scores(j + 1, 1)
        softmax_pv(j, 0, None)
        scores(j + 2, 0)
        softmax_pv(j + 1, 1, None)
        return carry

    assert n_diag == 2
    lax.fori_loop(0, n_full // 2, two_steps, 0)
    scores(n_full + 1, 1)
    softmax_pv(n_full, 0, 0)
    softmax_pv(n_full + 1, 1, 1)

    for g in range(g_heads):
        gate = gate_ref[0, g * d:(g + 1) * d, :].astype(F32)
        y = acc_ref[g, 0:d, :] / acc_ref[g, d:d + 1, :]
        o_ref[0, g * d:(g + 1) * d, :] = (y * (gate * jax.nn.sigmoid(gate))).astype(o_ref.dtype)


def _fox_attention(pt, qx, kx, *, n_heads, g_heads, tq, tk):
    bsz, _, s = pt.shape
    d = HEAD_DIM
    rows = g_heads * d
    nw = n_heads // g_heads
    blocks = 3 * rows * tq * 2 + 2 * rows * s * 2 + g_heads * XROWS * (tq + s) * 2
    scratch = g_heads * ((d + XROWS) * tq * 2 + 8 * tq * 4 + (d + XROWS) * tq * 4 + 2 * tk * tq * 4)
    return pl.pallas_call(
        functools.partial(_fox_attn_kernel, g_heads=g_heads, tq=tq, tk=tk),
        grid=(bsz, nw, s // tq),
        in_specs=[
            pl.BlockSpec((1, rows, tq), lambda b, hg, qi: (b, hg, qi)),
            pl.BlockSpec((1, rows, s), lambda b, hg, qi: (b, nw + hg, 0)),
            pl.BlockSpec((1, rows, s), lambda b, hg, qi: (b, 2 * nw + hg, 0)),
            pl.BlockSpec((1, rows, tq), lambda b, hg, qi: (b, 3 * nw + hg, qi)),
            pl.BlockSpec((1, g_heads, XROWS, tq), lambda b, hg, qi: (b, hg, 0, qi)),
            pl.BlockSpec((1, g_heads, XROWS, s), lambda b, hg, qi: (b, hg, 0, 0)),
        ],
        out_specs=pl.BlockSpec((1, rows, tq), lambda b, hg, qi: (b, hg, qi)),
        out_shape=jax.ShapeDtypeStruct((bsz, n_heads * d, s), BF16),
        scratch_shapes=[
            pltpu.VMEM((g_heads, d + XROWS, tq), BF16),
            pltpu.VMEM((g_heads, tk, tq), F32),
            pltpu.VMEM((g_heads, tk, tq), F32),
            pltpu.VMEM((g_heads, 1, tq), F32),
            pltpu.VMEM((g_heads, d + XROWS, tq), F32),
        ],
        compiler_params=pltpu.CompilerParams(
            dimension_semantics=("parallel", "parallel", "arbitrary"),
            vmem_limit_bytes=_vmem_limit(blocks, scratch, 4 * g_heads * tq * tk * 4),
        ),
        name="fox_attention",
    )(pt, pt, pt, pt, qx, kx)


def _swa_attn_kernel(q_ref, k_ref, v_ref, gate_ref, sink_ref, o_ref, *, blk, nsub):
    n = pl.program_id(2)
    d = HEAD_DIM
    g = q_ref.shape[1] // d
    ones_rows = (lax.broadcasted_iota(jnp.int32, (XROWS, 2 * blk), 0) == 0).astype(BF16)
    j_loc = lax.broadcasted_iota(jnp.int32, (2 * blk, g * blk), 0)
    t_loc = lax.broadcasted_iota(jnp.int32, (2 * blk, g * blk), 1) & (blk - 1)
    sink = sink_ref[0]

    for i in range(nsub):
        gb = n * nsub + i
        start = pl.multiple_of(jnp.maximum(gb - 1, 0) * blk, blk)
        diff = t_loc - j_loc + jnp.where(gb == 0, 0, blk)
        qcat = jnp.concatenate([q_ref[0, h * d:(h + 1) * d, i * blk:(i + 1) * blk] for h in range(g)], axis=1)
        s = lax.dot_general(k_ref[0, :, pl.ds(start, 2 * blk)], qcat, (((0,), (0,)), ((), ())),
                            preferred_element_type=F32)
        s = jnp.where((diff >= 0) & (diff < SWA_WINDOW), s, NEG_INF)
        m = jnp.maximum(jnp.max(s, axis=0, keepdims=True), sink)
        e = jnp.exp2(s - m).astype(BF16)
        va = jnp.concatenate([v_ref[0, :, pl.ds(start, 2 * blk)], ones_rows], axis=0)
        o = jnp.dot(va, e, preferred_element_type=F32)
        denom = o[d:d + 1] + jnp.exp2(sink - m)
        y = o[0:d] / denom
        for h in range(g):
            gate = gate_ref[0, h * d:(h + 1) * d, i * blk:(i + 1) * blk].astype(F32)
            o_ref[0, h * d:(h + 1) * d, i * blk:(i + 1) * blk] = (
                y[:, h * blk:(h + 1) * blk] * (gate * jax.nn.sigmoid(gate))).astype(o_ref.dtype)


def _swa_attention(pt, sink_rows, *, blk, nsub):
    bsz, _, s = pt.shape
    d = HEAD_DIM
    rows = SWA_GROUP * d
    wq, wk = SWA_Q_HEADS * d, SWA_KV_HEADS * d
    tq = nsub * blk
    k0, v0, g0 = wq // d, (wq + wk) // d, (wq + 2 * wk) // rows
    blocks = 3 * rows * tq * 2 + 2 * d * s * 2 + 8 * SWA_GROUP * blk * 4
    return pl.pallas_call(
        functools.partial(_swa_attn_kernel, blk=blk, nsub=nsub),
        grid=(bsz, SWA_KV_HEADS, s // tq),
        in_specs=[
            pl.BlockSpec((1, rows, tq), lambda b, h, n: (b, h, n)),
            pl.BlockSpec((1, d, s), lambda b, h, n: (b, k0 + h, 0)),
            pl.BlockSpec((1, d, s), lambda b, h, n: (b, v0 + h, 0)),
            pl.BlockSpec((1, rows, tq), lambda b, h, n: (b, g0 + h, n)),
            pl.BlockSpec((1, 1, SWA_GROUP * blk), lambda b, h, n: (h, 0, 0)),
        ],
        out_specs=pl.BlockSpec((1, rows, tq), lambda b, h, n: (b, h, n)),
        out_shape=jax.ShapeDtypeStruct((bsz, wq, s), BF16),
        compiler_params=pltpu.CompilerParams(
            dimension_semantics=("parallel", "parallel", "arbitrary"),
            vmem_limit_bytes=_vmem_limit(blocks, 0, 8 * nsub * 2 * blk * SWA_GROUP * blk * 4),
        ),
        name="swa_attention",
    )(pt, pt, pt, pt, sink_rows)


def _out_proj_kernel(zt_ref, x_ref, w_ref, fg_ref, o_ref, *, final_norm):
    out = x_ref[...] + lax.dot_general(zt_ref[0], w_ref[...], (((0,), (0,)), ((), ())),
                                       preferred_element_type=F32)
    if final_norm:
        ms = jnp.mean(out * out, axis=-1, keepdims=True)
        out = (out * lax.rsqrt(ms + RMS_EPS)) * fg_ref[...]
    o_ref[...] = out


def _out_proj(zt, x2d, w, fg, *, tm, final_norm):
    bsz, wid, s = zt.shape
    m, d = x2d.shape
    sb = s // tm
    blocks = wid * tm * 2 + 2 * tm * d * 4 + wid * d * 2 + d * 4
    return pl.pallas_call(
        functools.partial(_out_proj_kernel, final_norm=final_norm),
        grid=(m // tm,),
        in_specs=[
            pl.BlockSpec((1, wid, tm), lambda i: (i // sb, 0, i % sb)),
            pl.BlockSpec((tm, d), lambda i: (i, 0)),
            pl.BlockSpec((wid, d), lambda i: (0, 0)),
            pl.BlockSpec((1, d), lambda i: (0, 0)),
        ],
        out_specs=pl.BlockSpec((tm, d), lambda i: (i, 0)),
        out_shape=jax.ShapeDtypeStruct((m, d), F32),
        compiler_params=pltpu.CompilerParams(
            dimension_semantics=("parallel",),
            vmem_limit_bytes=_vmem_limit(blocks, 0, wid * tm * 2 + 3 * tm * d * 4),
        ),
        name="out_proj",
    )(zt, x2d, w, fg)


def _rope_tables(s):
    inv_freq = ROPE_THETA ** (-jnp.arange(ROT_HALF, dtype=F32) / ROT_HALF)
    ang = jnp.arange(s, dtype=F32)[:, None] * inv_freq[None, :]
    return jnp.cos(ang).T, jnp.sin(ang).T


def kernel(x, norm_g, fox_w_in, fox_b_f, fox_w_out, swa_w_in, swa_sinks, swa_w_out, final_g):
    bsz, s, d = x.shape
    x2d = x.reshape(bsz * s, d)

    wid = FOX_HEADS * HEAD_DIM
    w_in = fox_w_in[0]
    wt = w_in[:, :4 * wid].T.astype(BF16)
    wft = jnp.pad(w_in[:, 4 * wid:].T, ((0, LANES - FOX_HEADS), (0, 0))).astype(BF16)
    pt, zt = _norm_proj(x2d, norm_g[0][None, :], wt, (wft,), bsz=bsz, mode="fox",
                        tm=512, tn=1024, q_blocks=wid // 1024)
    qx, kx = _decay_rows(zt, fox_b_f[0][:, None], FOX_HEADS)
    yt = _fox_attention(pt, qx, kx, n_heads=FOX_HEADS, g_heads=4, tq=512, tk=256)
    x2d = _out_proj(yt, x2d, fox_w_out[0].astype(BF16), final_g[None, :], tm=512, final_norm=False)

    wq = SWA_Q_HEADS * HEAD_DIM
    wt = swa_w_in[0].T.astype(BF16)
    (pt,) = _norm_proj(x2d, norm_g[1][None, :], wt, _rope_tables(s), bsz=bsz, mode="swa",
                       tm=512, tn=512, q_blocks=wq // 512)
    sink_rows = jnp.repeat(swa_sinks[0].reshape(SWA_KV_HEADS, SWA_GROUP) * LOG2E, SWA_WINDOW, axis=1)[:, None, :]
    yt = _swa_attention(pt, sink_rows, blk=SWA_WINDOW, nsub=4)
    out = _out_proj(yt, x2d, swa_w_out[0].astype(BF16), final_g[None, :], tm=512, final_norm=True)
    return out.reshape(bsz, s, d)
```

```python
import functools
import math

import jax
import jax.numpy as jnp
from jax import lax
from jax.experimental import pallas as pl
from jax.experimental.pallas import tpu as pltpu

F32 = jnp.float32
BF16 = jnp.bfloat16

RMS_EPS = 1e-6
NEG_INF = -1e30
LOG2E = 1.4426950408889634
HEAD_DIM = 64
FOX_HEADS = 32
SWA_Q_HEADS = 32
SWA_KV_HEADS = 4
SWA_GROUP = SWA_Q_HEADS // SWA_KV_HEADS
SWA_WINDOW = 128
ROPE_THETA = 500000.0
ROT_DIM = HEAD_DIM // 4
ROT_HALF = ROT_DIM // 2
XROWS = 16
LANES = 128
VMEM_CAPACITY = 64 * 1024 * 1024
QK_SCALE = HEAD_DIM ** -0.5 * LOG2E


def _vmem_limit(pipelined_block_bytes, scratch_bytes, temp_bytes):
    need = 2 * pipelined_block_bytes + scratch_bytes + temp_bytes
    return int(min(need, VMEM_CAPACITY - (4 << 20)))


def _rope_rows(res, cos, sin, base):
    x1 = res[base:base + ROT_HALF]
    x2 = res[base + ROT_HALF:base + ROT_DIM]
    return jnp.concatenate([x1 * cos - x2 * sin, x2 * cos + x1 * sin], axis=0)


_NT = (((1,), (1,)), ((), ()))


def _norm_to_scratch(x_ref, g_ref, h_ref):
    x = x_ref[...]
    ms = jnp.mean(x * x, axis=-1, keepdims=True)
    h_ref[...] = ((x * lax.rsqrt(ms + RMS_EPS)) * g_ref[...]).astype(BF16)


def _norm_proj_fox_kernel(x_ref, g_ref, wt_ref, wft_ref, o_ref, of_ref, h_ref, *, q_blocks):
    j = pl.program_id(1)

    @pl.when(j == 0)
    def _():
        _norm_to_scratch(x_ref, g_ref, h_ref)
        of_ref[0] = lax.dot_general(wft_ref[...], h_ref[...], _NT, preferred_element_type=F32)

    res = lax.dot_general(wt_ref[...], h_ref[...], _NT, preferred_element_type=F32)
    o_ref[0] = (res * jnp.where(j < q_blocks, QK_SCALE, 1.0)).astype(o_ref.dtype)


def _norm_proj_swa_kernel(x_ref, g_ref, wt_ref, cos_ref, sin_ref, o_ref, h_ref, *, q_rows, k_rows):
    j = pl.program_id(1)
    tn = o_ref.shape[1]

    @pl.when(j == 0)
    def _():
        _norm_to_scratch(x_ref, g_ref, h_ref)

    res = lax.dot_general(wt_ref[...], h_ref[...], _NT, preferred_element_type=F32)

    def write_block(jj):
        cos, sin = cos_ref[...], sin_ref[...]
        for base in range(0, tn, HEAD_DIM):
            row = jj * tn + base
            if row < q_rows + k_rows:
                scale = QK_SCALE if row < q_rows else 1.0
                o_ref[0, base:base + ROT_DIM, :] = (_rope_rows(res, cos, sin, base) * scale).astype(o_ref.dtype)
                o_ref[0, base + ROT_DIM:base + HEAD_DIM, :] = (
                    res[base + ROT_DIM:base + HEAD_DIM] * scale).astype(o_ref.dtype)
            else:
                o_ref[0, base:base + HEAD_DIM, :] = res[base:base + HEAD_DIM].astype(o_ref.dtype)

    n_rope_blocks = pl.cdiv(q_rows + k_rows, tn)
    for jj in range(n_rope_blocks):
        pl.when(j == jj)(functools.partial(write_block, jj))

    @pl.when(j >= n_rope_blocks)
    def _():
        o_ref[0] = res.astype(o_ref.dtype)


def _norm_proj(x2d, g, wt, extra, *, bsz, mode, tm, tn, **sections):
    m, d = x2d.shape
    n = wt.shape[0]
    s = m // bsz
    sb = s // tm
    in_specs = [
        pl.BlockSpec((tm, d), lambda i, j: (i, 0)),
        pl.BlockSpec((1, d), lambda i, j: (0, 0)),
        pl.BlockSpec((tn, d), lambda i, j: (j, 0)),
    ]
    out_specs = [pl.BlockSpec((1, tn, tm), lambda i, j: (i // sb, j, i % sb))]
    out_shape = [jax.ShapeDtypeStruct((bsz, n, s), BF16)]
    blocks = tm * d * 4 + d * 4 + tn * d * 2 + tn * tm * 2
    if mode == "fox":
        body = _norm_proj_fox_kernel
        in_specs.append(pl.BlockSpec((LANES, d), lambda i, j: (0, 0)))
        out_specs.append(pl.BlockSpec((1, LANES, tm), lambda i, j: (i // sb, 0, i % sb)))
        out_shape.append(jax.ShapeDtypeStruct((bsz, LANES, s), F32))
        blocks += LANES * d * 2 + LANES * tm * 4
    else:
        body = _norm_proj_swa_kernel
        in_specs += [pl.BlockSpec((ROT_HALF, tm), lambda i, j: (0, i % sb))] * 2
        blocks += 2 * ROT_HALF * tm * 4
    return pl.pallas_call(
        functools.partial(body, **sections),
        grid=(m // tm, n // tn),
        in_specs=in_specs,
        out_specs=out_specs,
        out_shape=out_shape,
        scratch_shapes=[pltpu.VMEM((tm, d), BF16)],
        compiler_params=pltpu.CompilerParams(
            dimension_semantics=("parallel", "arbitrary"),
            vmem_limit_bytes=_vmem_limit(blocks, tm * d * 2, 3 * tm * d * 4 + 2 * tn * tm * 4),
        ),
        name="norm_proj_" + mode,
    )(x2d, g, wt, *extra)


def _decay_rows_kernel(z_ref, b_ref, qx_ref, kx_ref):
    z = z_ref[0] + b_ref[...]
    x = (jnp.minimum(z, 0.0) - jnp.log1p(jnp.exp(-jnp.abs(z)))) * LOG2E
    n_heads, s = x.shape
    col = lax.broadcasted_iota(jnp.int32, x.shape, 1)
    shift = 1
    while shift < s:
        x = x + jnp.where(col >= shift, pltpu.roll(x, shift, axis=1), 0.0)
        shift *= 2
    hi = x.astype(BF16).astype(F32)
    mid = (x - hi).astype(BF16).astype(F32)
    lo = (x - hi - mid).astype(BF16).astype(F32)
    row = lax.broadcasted_iota(jnp.int32, (XROWS, s), 0)
    for h in range(n_heads):
        parts_q = jnp.where(row == 0, hi[h:h + 1], jnp.where(row == 1, mid[h:h + 1], lo[h:h + 1]))
        parts_k = jnp.where(row == 3, hi[h:h + 1], jnp.where(row == 4, mid[h:h + 1], lo[h:h + 1]))
        qx = jnp.where(row < 3, parts_q, jnp.where(row < 6, 1.0, 0.0))
        kx = jnp.where(row < 3, 1.0, jnp.where(row < 6, -parts_k, 0.0))
        qx_ref[0, h] = qx.astype(BF16)
        kx_ref[0, h] = kx.astype(BF16)


def _decay_rows(z, b, n_heads):
    bsz, _, s = z.shape
    out_blk = n_heads * XROWS * s * 2
    return pl.pallas_call(
        _decay_rows_kernel,
        grid=(bsz,),
        in_specs=[
            pl.BlockSpec((1, n_heads, s), lambda i: (i, 0, 0)),
            pl.BlockSpec((n_heads, 1), lambda i: (0, 0)),
        ],
        out_specs=[pl.BlockSpec((1, n_heads, XROWS, s), lambda i: (i, 0, 0, 0))] * 2,
        out_shape=[jax.ShapeDtypeStruct((bsz, n_heads, XROWS, s), BF16)] * 2,
        compiler_params=pltpu.CompilerParams(
            dimension_semantics=("parallel",),
            vmem_limit_bytes=_vmem_limit(n_heads * s * 4 + 2 * out_blk, 0, 8 * n_heads * s * 4),
        ),
        name="decay_rows",
    )(z, b)


def _fox_attn_kernel(q_ref, k_ref, v_ref, gate_ref, qx_ref, kx_ref, o_ref,
                     qa_ref, s0_ref, s1_ref, bm0_ref, bm1_ref, m_ref, acc_ref, *, g_heads, tq, tk):
    qi = pl.program_id(2)
    n_diag = tq // tk
    n_full = qi * n_diag
    d = HEAD_DIM
    s_refs = (s0_ref, s1_ref)
    bm_refs = (bm0_ref, bm1_ref)

    for g in range(g_heads):
        qa_ref[g, 0:d, :] = q_ref[0, g * d:(g + 1) * d, :]
        qa_ref[g, d:d + XROWS, :] = qx_ref[0, g]
    m_ref[...] = jnp.full(m_ref.shape, NEG_INF, F32)
    acc_ref[...] = jnp.zeros(acc_ref.shape, F32)

    ones_rows = (lax.broadcasted_iota(jnp.int32, (XROWS, tk), 0) == 0).astype(BF16)

    def scores(j, slot, g):
        off = pl.multiple_of(j * tk, tk)
        ka = jnp.concatenate([k_ref[0, g * d:(g + 1) * d, pl.ds(off, tk)],
                              kx_ref[0, g, :, pl.ds(off, tk)]], axis=0)
        s = lax.dot_general(ka, qa_ref[g], (((0,), (0,)), ((), ())), preferred_element_type=F32)
        s_refs[slot][g] = s
        bm_refs[slot][g] = jnp.max(s, axis=0, keepdims=True)

    def softmax_pv(j, slot, g, diag):
        off = pl.multiple_of(j * tk, tk)
        s = s_refs[slot][g]
        if diag is None:
            bm = bm_refs[slot][g]
        else:
            kpos = diag * tk + lax.broadcasted_iota(jnp.int32, (tk, tq), 0)
            qpos = lax.broadcasted_iota(jnp.int32, (tk, tq), 1)
            s = jnp.where(kpos <= qpos, s, NEG_INF)
            bm = jnp.max(s, axis=0, keepdims=True)
        m_prev = m_ref[g]
        m_new = jnp.maximum(m_prev, bm)
        alpha = jnp.exp2(m_prev - m_new)
        p = jnp.exp2(s - m_new).astype(BF16)
        va = jnp.concatenate([v_ref[0, g * d:(g + 1) * d, pl.ds(off, tk)], ones_rows], axis=0)
        acc_ref[g] = alpha * acc_ref[g] + jnp.dot(va, p, preferred_element_type=F32)
        m_ref[g] = m_new

    def step(j, slot, diag, with_next=True):
        for g in range(g_heads):
            if with_next:
                scores(j + 1, 1 - slot, g)
            softmax_pv(j, slot, g, diag)

    for g in range(g_heads):
        scores(0, 0, g)

    def two_steps(jj, carry):
        step(2 * jj, 0, None)
        step(2 * jj + 1, 1, None)
        return carry

    assert n_diag == 2
    lax.fori_loop(0, n_full // 2, two_steps, 0)
    step(n_full, 0, 0)
    step(n_full + 1, 1, 1, with_next=False)

    for g in range(g_heads):
        gate = gate_ref[0, g * d:(g + 1) * d, :].astype(F32)
        y = acc_ref[g, 0:d, :] / acc_ref[g, d:d + 1, :]
        o_ref[0, g * d:(g + 1) * d, :] = (y * (gate * jax.nn.sigmoid(gate))).astype(o_ref.dtype)


def _fox_attention(pt, qx, kx, *, n_heads, g_heads, tq, tk):
    bsz, _, s = pt.shape
    d = HEAD_DIM
    rows = g_heads * d
    nw = n_heads // g_heads
    blocks = 3 * rows * tq * 2 + 2 * rows * s * 2 + g_heads * XROWS * (tq + s) * 2
    scratch = g_heads * ((d + XROWS) * tq * 2 + 3 * 8 * tq * 4 + (d + XROWS) * tq * 4 + 2 * tk * tq * 4)
    return pl.pallas_call(
        functools.partial(_fox_attn_kernel, g_heads=g_heads, tq=tq, tk=tk),
        grid=(bsz, nw, s // tq),
        in_specs=[
            pl.BlockSpec((1, rows, tq), lambda b, hg, qi: (b, hg, qi)),
            pl.BlockSpec((1, rows, s), lambda b, hg, qi: (b, nw + hg, 0)),
            pl.BlockSpec((1, rows, s), lambda b, hg, qi: (b, 2 * nw + hg, 0)),
            pl.BlockSpec((1, rows, tq), lambda b, hg, qi: (b, 3 * nw + hg, qi)),
            pl.BlockSpec((1, g_heads, XROWS, tq), lambda b, hg, qi: (b, hg, 0, qi)),
            pl.BlockSpec((1, g_heads, XROWS, s), lambda b, hg, qi: (b, hg, 0, 0)),
        ],
        out_specs=pl.BlockSpec((1, rows, tq), lambda b, hg, qi: (b, hg, qi)),
        out_shape=jax.ShapeDtypeStruct((bsz, n_heads * d, s), BF16),
        scratch_shapes=[
            pltpu.VMEM((g_heads, d + XROWS, tq), BF16),
            pltpu.VMEM((g_heads, tk, tq), F32),
            pltpu.VMEM((g_heads, tk, tq), F32),
            pltpu.VMEM((g_heads, 1, tq), F32),
            pltpu.VMEM((g_heads, 1, tq), F32),
            pltpu.VMEM((g_heads, 1, tq), F32),
            pltpu.VMEM((g_heads, d + XROWS, tq), F32),
        ],
        compiler_params=pltpu.CompilerParams(
            dimension_semantics=("parallel", "parallel", "arbitrary"),
            vmem_limit_bytes=_vmem_limit(blocks, scratch, 4 * g_heads * tq * tk * 4),
        ),
        name="fox_attention",
    )(pt, pt, pt, pt, qx, kx)


def _swa_attn_kernel(q_ref, k_ref, v_ref, gate_ref, sink_ref, o_ref, *, blk, nsub):
    n = pl.program_id(2)
    d = HEAD_DIM
    g = q_ref.shape[1] // d
    ones_rows = (lax.broadcasted_iota(jnp.int32, (XROWS, 2 * blk), 0) == 0).astype(BF16)
    j_loc = lax.broadcasted_iota(jnp.int32, (2 * blk, blk), 0)
    t_loc = lax.broadcasted_iota(jnp.int32, (2 * blk, blk), 1)
    sink = sink_ref[0]

    def window_start(i):
        return pl.multiple_of(jnp.maximum(n * nsub + i - 1, 0) * blk, blk)

    scores = []
    for i in range(nsub):
        qcat = jnp.concatenate([q_ref[0, h * d:(h + 1) * d, i * blk:(i + 1) * blk] for h in range(g)], axis=1)
        scores.append(lax.dot_general(k_ref[0, :, pl.ds(window_start(i), 2 * blk)], qcat,
                                      (((0,), (0,)), ((), ())), preferred_element_type=F32))

    def band_bias(first):
        diff = t_loc - j_loc + (0 if first else blk)
        return jnp.where((diff >= 0) & (diff < SWA_WINDOW), 0.0, NEG_INF)

    bias = band_bias(False)

    for i in range(nsub):
        start = window_start(i)
        bias_i = jnp.where(n == 0, band_bias(True), bias) if i == 0 else bias
        s = jnp.concatenate([scores[i][:, h * blk:(h + 1) * blk] + bias_i for h in range(g)], axis=1)
        m = jnp.maximum(jnp.max(s, axis=0, keepdims=True), sink)
        e = jnp.exp2(s - m).astype(BF16)
        va = jnp.concatenate([v_ref[0, :, pl.ds(start, 2 * blk)], ones_rows], axis=0)
        o = jnp.dot(va, e, preferred_element_type=F32)
        denom = o[d:d + 1] + jnp.exp2(sink - m)
        y = o[0:d] / denom
        for h in range(g):
            gate = gate_ref[0, h * d:(h + 1) * d, i * blk:(i + 1) * blk].astype(F32)
            o_ref[0, h * d:(h + 1) * d, i * blk:(i + 1) * blk] = (
                y[:, h * blk:(h + 1) * blk] * (gate * jax.nn.sigmoid(gate))).astype(o_ref.dtype)


def _swa_attention(pt, sink_rows, *, blk, nsub):
    bsz, _, s = pt.shape
    d = HEAD_DIM
    rows = SWA_GROUP * d
    wq, wk = SWA_Q_HEADS * d, SWA_KV_HEADS * d
    tq = nsub * blk
    k0, v0, g0 = wq // d, (wq + wk) // d, (wq + 2 * wk) // rows
    blocks = 3 * rows * tq * 2 + 2 * d * s * 2 + 8 * SWA_GROUP * blk * 4
    return pl.pallas_call(
        functools.partial(_swa_attn_kernel, blk=blk, nsub=nsub),
        grid=(bsz, SWA_KV_HEADS, s // tq),
        in_specs=[
            pl.BlockSpec((1, rows, tq), lambda b, h, n: (b, h, n)),
            pl.BlockSpec((1, d, s), lambda b, h, n: (b, k0 + h, 0)),
            pl.BlockSpec((1, d, s), lambda b, h, n: (b, v0 + h, 0)),
            pl.BlockSpec((1, rows, tq), lambda b, h, n: (b, g0 + h, n)),
            pl.BlockSpec((1, 1, SWA_GROUP * blk), lambda b, h, n: (h, 0, 0)),
        ],
        out_specs=pl.BlockSpec((1, rows, tq), lambda b, h, n: (b, h, n)),
        out_shape=jax.ShapeDtypeStruct((bsz, wq, s), BF16),
        compiler_params=pltpu.CompilerParams(
            dimension_semantics=("parallel", "parallel", "arbitrary"),
            vmem_limit_bytes=_vmem_limit(blocks, 0, 8 * nsub * 2 * blk * SWA_GROUP * blk * 4),
        ),
        name="swa_attention",
    )(pt, pt, pt, pt, sink_rows)


def _out_proj_kernel(zt_ref, x_ref, w_ref, fg_ref, o_ref, *, final_norm):
    out = x_ref[...] + lax.dot_general(zt_ref[0], w_ref[...], (((0,), (0,)), ((), ())),
                                       preferred_element_type=F32)
    if final_norm:
        ms = jnp.mean(out * out, axis=-1, keepdims=True)
        out = (out * lax.rsqrt(ms + RMS_EPS)) * fg_ref[...]
    o_ref[...] = out


def _out_proj(zt, x2d, w, fg, *, tm, final_norm):
    bsz, wid, s = zt.shape
    m, d = x2d.shape
    sb = s // tm
    blocks = wid * tm * 2 + 2 * tm * d * 4 + wid * d * 2 + d * 4
    return pl.pallas_call(
        functools.partial(_out_proj_kernel, final_norm=final_norm),
        grid=(m // tm,),
        in_specs=[
            pl.BlockSpec((1, wid, tm), lambda i: (i // sb, 0, i % sb)),
            pl.BlockSpec((tm, d), lambda i: (i, 0)),
            pl.BlockSpec((wid, d), lambda i: (0, 0)),
            pl.BlockSpec((1, d), lambda i: (0, 0)),
        ],
        out_specs=pl.BlockSpec((tm, d), lambda i: (i, 0)),
        out_shape=jax.ShapeDtypeStruct((m, d), F32),
        compiler_params=pltpu.CompilerParams(
            dimension_semantics=("parallel",),
            vmem_limit_bytes=_vmem_limit(blocks, 0, wid * tm * 2 + 3 * tm * d * 4),
        ),
        name="out_proj",
    )(zt, x2d, w, fg)


def _rope_tables(s):
    inv_freq = ROPE_THETA ** (-jnp.arange(ROT_HALF, dtype=F32) / ROT_HALF)
    ang = jnp.arange(s, dtype=F32)[:, None] * inv_freq[None, :]
    return jnp.cos(ang).T, jnp.sin(ang).T


def kernel(x, norm_g, fox_w_in, fox_b_f, fox_w_out, swa_w_in, swa_sinks, swa_w_out, final_g):
    bsz, s, d = x.shape
    x2d = x.reshape(bsz * s, d)

    wid = FOX_HEADS * HEAD_DIM
    w_in = fox_w_in[0]
    wt = w_in[:, :4 * wid].T.astype(BF16)
    wft = jnp.pad(w_in[:, 4 * wid:].T, ((0, LANES - FOX_HEADS), (0, 0))).astype(BF16)
    pt, zt = _norm_proj(x2d, norm_g[0][None, :], wt, (wft,), bsz=bsz, mode="fox",
                        tm=512, tn=2048, q_blocks=wid // 2048)
    qx, kx = _decay_rows(zt, fox_b_f[0][:, None], FOX_HEADS)
    yt = _fox_attention(pt, qx, kx, n_heads=FOX_HEADS, g_heads=8, tq=512, tk=256)
    x2d = _out_proj(yt, x2d, fox_w_out[0].astype(BF16), final_g[None, :], tm=512, final_norm=False)

    wq = SWA_Q_HEADS * HEAD_DIM
    wt = swa_w_in[0].T.astype(BF16)
    (pt,) = _norm_proj(x2d, norm_g[1][None, :], wt, _rope_tables(s), bsz=bsz, mode="swa",
                       tm=512, tn=1536, q_rows=wq, k_rows=SWA_KV_HEADS * HEAD_DIM)
    sink_rows = jnp.repeat(swa_sinks[0].reshape(SWA_KV_HEADS, SWA_GROUP) * LOG2E, SWA_WINDOW, axis=1)[:, None, :]
    yt = _swa_attention(pt, sink_rows, blk=SWA_WINDOW, nsub=4)
    out = _out_proj(yt, x2d, swa_w_out[0].astype(BF16), final_g[None, :], tm=512, final_norm=True)
    return out.reshape(bsz, s, d)
```

```python
import functools
import math

import jax
import jax.numpy as jnp
from jax import lax
from jax.experimental import pallas as pl
from jax.experimental.pallas import tpu as pltpu

F32 = jnp.float32
BF16 = jnp.bfloat16

RMS_EPS = 1e-6
NEG_INF = -1e30
LOG2E = 1.4426950408889634
HEAD_DIM = 64
FOX_HEADS = 32
SWA_Q_HEADS = 32
SWA_KV_HEADS = 4
SWA_GROUP = SWA_Q_HEADS // SWA_KV_HEADS
SWA_WINDOW = 128
ROPE_THETA = 500000.0
ROT_DIM = HEAD_DIM // 4
ROT_HALF = ROT_DIM // 2
XROWS = 16
LANES = 128
VMEM_CAPACITY = 64 * 1024 * 1024
QK_SCALE = HEAD_DIM ** -0.5 * LOG2E


def _silu(x):
    xf = x.astype(F32)
    return (0.5 * xf) * (1.0 + jnp.tanh(0.5 * xf))


def _vmem_limit(pipelined_block_bytes, scratch_bytes, temp_bytes):
    need = 2 * pipelined_block_bytes + scratch_bytes + temp_bytes
    return int(min(need, VMEM_CAPACITY - (4 << 20)))


_NT = (((1,), (1,)), ((), ()))


def _norm_to_scratch(x_ref, g_ref, h_ref):
    x = x_ref[...]
    ms = jnp.mean(x * x, axis=-1, keepdims=True)
    h_ref[...] = ((x * lax.rsqrt(ms + RMS_EPS)) * g_ref[...]).astype(BF16)


def _norm_proj_fox_kernel(x_ref, g_ref, wt_ref, wft_ref, o_ref, of_ref, h_ref, *, q_blocks):
    j = pl.program_id(1)

    @pl.when(j == 0)
    def _():
        _norm_to_scratch(x_ref, g_ref, h_ref)
        of_ref[0] = lax.dot_general(wft_ref[...], h_ref[...], _NT, preferred_element_type=F32)

    res = lax.dot_general(wt_ref[...], h_ref[...], _NT, preferred_element_type=F32)
    o_ref[0] = (res * jnp.where(j < q_blocks, QK_SCALE, 1.0)).astype(o_ref.dtype)


def _norm_proj_swa_kernel(scale_ref, rope_ref, x_ref, g_ref, wt_ref, cos_ref, sin_ref, o_ref, h_ref):
    j = pl.program_id(1)
    tn = o_ref.shape[1]
    heads = tn // HEAD_DIM

    @pl.when(j == 0)
    def _():
        _norm_to_scratch(x_ref, g_ref, h_ref)

    res = lax.dot_general(wt_ref[...], h_ref[...], _NT, preferred_element_type=F32)
    cos, sin = cos_ref[...], sin_ref[...]
    for hh in range(heads):
        base = hh * HEAD_DIM
        scale = scale_ref[j * heads + hh]
        rope = rope_ref[j * heads + hh]
        ch = (1.0 + rope * (cos - 1.0)) * scale
        sh = (rope * sin) * scale
        x1 = res[base:base + ROT_HALF]
        x2 = res[base + ROT_HALF:base + ROT_DIM]
        o_ref[0, base:base + ROT_DIM, :] = jnp.concatenate(
            [x1 * ch - x2 * sh, x2 * ch + x1 * sh], axis=0).astype(o_ref.dtype)
        o_ref[0, base + ROT_DIM:base + HEAD_DIM, :] = (res[base + ROT_DIM:base + HEAD_DIM] * scale).astype(o_ref.dtype)


def _norm_proj(x2d, g, wt, *, bsz, n, mode, tm, tn, q_blocks=None, tables=None):
    m, d = x2d.shape
    s = m // bsz
    sb = s // tm
    in_specs = [
        pl.BlockSpec((tm, d), lambda i, j: (i, 0)),
        pl.BlockSpec((1, d), lambda i, j: (0, 0)),
        pl.BlockSpec((tn, d), lambda i, j: (j, 0)),
    ]
    args = [x2d, g, wt]
    out_specs = [pl.BlockSpec((1, tn, tm), lambda i, j: (i // sb, j, i % sb))]
    out_shape = [jax.ShapeDtypeStruct((bsz, n, s), BF16)]
    blocks = tm * d * 4 + d * 4 + tn * d * 2 + tn * tm * 2
    if mode == "fox":
        body = functools.partial(_norm_proj_fox_kernel, q_blocks=q_blocks)
        in_specs.append(pl.BlockSpec((LANES, d), lambda i, j: (n // LANES, 0)))
        args.append(wt)
        out_specs.append(pl.BlockSpec((1, LANES, tm), lambda i, j: (i // sb, 0, i % sb)))
        out_shape.append(jax.ShapeDtypeStruct((bsz, LANES, s), F32))
        blocks += LANES * d * 2 + LANES * tm * 4
    else:
        body = _norm_proj_swa_kernel
        scale, rope, cos, sin = tables
        in_specs = [pl.BlockSpec(memory_space=pltpu.SMEM)] * 2 + in_specs
        in_specs += [pl.BlockSpec((ROT_HALF, tm), lambda i, j: (0, i % sb))] * 2
        args = [scale, rope] + args + [cos, sin]
        blocks += 2 * ROT_HALF * tm * 4
    return pl.pallas_call(
        body,
        grid=(m // tm, n // tn),
        in_specs=in_specs,
        out_specs=out_specs,
        out_shape=out_shape,
        scratch_shapes=[pltpu.VMEM((tm, d), BF16)],
        compiler_params=pltpu.CompilerParams(
            dimension_semantics=("parallel", "arbitrary"),
            vmem_limit_bytes=_vmem_limit(blocks, tm * d * 2, 3 * tm * d * 4 + 2 * tn * tm * 4),
        ),
        name="norm_proj_" + mode,
    )(*args)


def _decay_rows_kernel(z_ref, b_ref, qx_ref, kx_ref):
    z = z_ref[0] + b_ref[...]
    x = (jnp.minimum(z, 0.0) - jnp.log1p(jnp.exp(-jnp.abs(z)))) * LOG2E
    n_heads, s = x.shape
    col = lax.broadcasted_iota(jnp.int32, x.shape, 1)
    shift = 1
    while shift < s:
        x = x + jnp.where(col >= shift, pltpu.roll(x, shift, axis=1), 0.0)
        shift *= 2
    hi = x.astype(BF16).astype(F32)
    mid = (x - hi).astype(BF16).astype(F32)
    lo = (x - hi - mid).astype(BF16).astype(F32)
    row = lax.broadcasted_iota(jnp.int32, (XROWS, s), 0)
    for h in range(n_heads):
        parts_q = jnp.where(row == 0, hi[h:h + 1], jnp.where(row == 1, mid[h:h + 1], lo[h:h + 1]))
        parts_k = jnp.where(row == 3, hi[h:h + 1], jnp.where(row == 4, mid[h:h + 1], lo[h:h + 1]))
        qx = jnp.where(row < 3, parts_q, jnp.where(row < 6, 1.0, 0.0))
        kx = jnp.where(row < 3, 1.0, jnp.where(row < 6, -parts_k, 0.0))
        qx_ref[0, h] = qx.astype(BF16)
        kx_ref[0, h] = kx.astype(BF16)


def _decay_rows(z, b, n_heads):
    bsz, _, s = z.shape
    out_blk = n_heads * XROWS * s * 2
    return pl.pallas_call(
        _decay_rows_kernel,
        grid=(bsz,),
        in_specs=[
            pl.BlockSpec((1, n_heads, s), lambda i: (i, 0, 0)),
            pl.BlockSpec((n_heads, 1), lambda i: (0, 0)),
        ],
        out_specs=[pl.BlockSpec((1, n_heads, XROWS, s), lambda i: (i, 0, 0, 0))] * 2,
        out_shape=[jax.ShapeDtypeStruct((bsz, n_heads, XROWS, s), BF16)] * 2,
        compiler_params=pltpu.CompilerParams(
            dimension_semantics=("parallel",),
            vmem_limit_bytes=_vmem_limit(n_heads * s * 4 + 2 * out_blk, 0, 8 * n_heads * s * 4),
        ),
        name="decay_rows",
    )(z, b)


def _fox_attn_kernel(q_ref, k_ref, v_ref, gate_ref, qx_ref, kx_ref, o_ref,
                     qa_ref, s0_ref, s1_ref, bm0_ref, bm1_ref, m_ref, acc_ref, *, g_heads, tq, tk):
    qi = pl.program_id(2)
    n_diag = tq // tk
    n_full = qi * n_diag
    d = HEAD_DIM
    s_refs = (s0_ref, s1_ref)
    bm_refs = (bm0_ref, bm1_ref)

    for g in range(g_heads):
        qa_ref[g, 0:d, :] = q_ref[0, g * d:(g + 1) * d, :]
        qa_ref[g, d:d + XROWS, :] = qx_ref[0, g]
    m_ref[...] = jnp.full(m_ref.shape, NEG_INF, F32)
    acc_ref[...] = jnp.zeros(acc_ref.shape, F32)

    ones_rows = (lax.broadcasted_iota(jnp.int32, (XROWS, tk), 0) == 0).astype(BF16)

    def scores(j, slot, g, col0=0):
        off = pl.multiple_of(j * tk, tk)
        ka = jnp.concatenate([k_ref[0, g * d:(g + 1) * d, pl.ds(off, tk)],
                              kx_ref[0, g, :, pl.ds(off, tk)]], axis=0)
        s = lax.dot_general(ka, qa_ref[g, :, col0:], (((0,), (0,)), ((), ())),
                            preferred_element_type=F32)
        s_refs[slot][g, :, col0:] = s
        bm_refs[slot][g, :, col0:] = jnp.max(s, axis=0, keepdims=True)

    def softmax_pv(j, slot, g, diag):
        col0 = 0 if diag is None else diag * tk
        off = pl.multiple_of(j * tk, tk)
        s = s_refs[slot][g, :, col0:]
        if diag is None:
            bm = bm_refs[slot][g]
        else:
            krow = lax.broadcasted_iota(jnp.int32, s.shape, 0)
            qcol = lax.broadcasted_iota(jnp.int32, s.shape, 1)
            s = jnp.where(krow <= qcol, s, NEG_INF)
            bm = jnp.max(s, axis=0, keepdims=True)
        m_prev = m_ref[g, :, col0:]
        m_new = jnp.maximum(m_prev, bm)
        alpha = jnp.exp2(m_prev - m_new)
        p = jnp.exp2(s - m_new).astype(BF16)
        va = jnp.concatenate([v_ref[0, g * d:(g + 1) * d, pl.ds(off, tk)], ones_rows], axis=0)
        acc_ref[g, :, col0:] = alpha * acc_ref[g, :, col0:] + jnp.dot(va, p, preferred_element_type=F32)
        m_ref[g, :, col0:] = m_new

    def step(j, slot, diag, next_col0=0):
        for g in range(g_heads):
            if next_col0 is not None:
                scores(j + 1, 1 - slot, g, next_col0)
            softmax_pv(j, slot, g, diag)

    for g in range(g_heads):
        scores(0, 0, g)

    def two_steps(jj, carry):
        step(2 * jj, 0, None)
        step(2 * jj + 1, 1, None)
        return carry

    assert n_diag == 2
    lax.fori_loop(0, n_full // 2, two_steps, 0)
    step(n_full, 0, 0, next_col0=tk)
    step(n_full + 1, 1, 1, next_col0=None)

    for g in range(g_heads):
        y = acc_ref[g, 0:d, :] / acc_ref[g, d:d + 1, :]
        o_ref[0, g * d:(g + 1) * d, :] = (y * _silu(gate_ref[0, g * d:(g + 1) * d, :])).astype(o_ref.dtype)


def _fox_attention(pt, qx, kx, *, n_heads, g_heads, tq, tk):
    bsz, _, s = pt.shape
    d = HEAD_DIM
    rows = g_heads * d
    nw = n_heads // g_heads
    blocks = 3 * rows * tq * 2 + 2 * rows * s * 2 + g_heads * XROWS * (tq + s) * 2
    scratch = g_heads * ((d + XROWS) * tq * 2 + 3 * 8 * tq * 4 + (d + XROWS) * tq * 4 + 2 * tk * tq * 4)
    return pl.pallas_call(
        functools.partial(_fox_attn_kernel, g_heads=g_heads, tq=tq, tk=tk),
        grid=(bsz, nw, s // tq),
        in_specs=[
            pl.BlockSpec((1, rows, tq), lambda b, hg, qi: (b, hg, qi)),
            pl.BlockSpec((1, rows, s), lambda b, hg, qi: (b, nw + hg, 0)),
            pl.BlockSpec((1, rows, s), lambda b, hg, qi: (b, 2 * nw + hg, 0)),
            pl.BlockSpec((1, rows, tq), lambda b, hg, qi: (b, 3 * nw + hg, qi)),
            pl.BlockSpec((1, g_heads, XROWS, tq), lambda b, hg, qi: (b, hg, 0, qi)),
            pl.BlockSpec((1, g_heads, XROWS, s), lambda b, hg, qi: (b, hg, 0, 0)),
        ],
        out_specs=pl.BlockSpec((1, rows, tq), lambda b, hg, qi: (b, hg, qi)),
        out_shape=jax.ShapeDtypeStruct((bsz, n_heads * d, s), BF16),
        scratch_shapes=[
            pltpu.VMEM((g_heads, d + XROWS, tq), BF16),
            pltpu.VMEM((g_heads, tk, tq), F32),
            pltpu.VMEM((g_heads, tk, tq), F32),
            pltpu.VMEM((g_heads, 1, tq), F32),
            pltpu.VMEM((g_heads, 1, tq), F32),
            pltpu.VMEM((g_heads, 1, tq), F32),
            pltpu.VMEM((g_heads, d + XROWS, tq), F32),
        ],
        compiler_params=pltpu.CompilerParams(
            dimension_semantics=("parallel", "parallel", "arbitrary"),
            vmem_limit_bytes=_vmem_limit(blocks, scratch, 4 * g_heads * tq * tk * 4),
        ),
        name="fox_attention",
    )(pt, pt, pt, pt, qx, kx)


def _swa_attn_kernel(q_ref, k_ref, v_ref, gate_ref, sink_ref, o_ref, *, blk, nsub):
    n = pl.program_id(2)
    d = HEAD_DIM
    g = q_ref.shape[1] // d
    ones_rows = (lax.broadcasted_iota(jnp.int32, (XROWS, 2 * blk), 0) == 0).astype(BF16)
    j_loc = lax.broadcasted_iota(jnp.int32, (2 * blk, blk), 0)
    t_loc = lax.broadcasted_iota(jnp.int32, (2 * blk, blk), 1)
    sink = sink_ref[0]

    def window_start(i):
        return pl.multiple_of(jnp.maximum(n * nsub + i - 1, 0) * blk, blk)

    scores = []
    for i in range(nsub):
        qcat = jnp.concatenate([q_ref[0, h * d:(h + 1) * d, i * blk:(i + 1) * blk] for h in range(g)], axis=1)
        scores.append(lax.dot_general(k_ref[0, :, pl.ds(window_start(i), 2 * blk)], qcat,
                                      (((0,), (0,)), ((), ())), preferred_element_type=F32))

    def band_bias(first):
        diff = t_loc - j_loc + (0 if first else blk)
        return jnp.where((diff >= 0) & (diff < SWA_WINDOW), 0.0, NEG_INF)

    bias = band_bias(False)

    for i in range(nsub):
        start = window_start(i)
        bias_i = jnp.where(n == 0, band_bias(True), bias) if i == 0 else bias
        s = jnp.concatenate([scores[i][:, h * blk:(h + 1) * blk] + bias_i for h in range(g)], axis=1)
        m = jnp.maximum(jnp.max(s, axis=0, keepdims=True), sink)
        e = jnp.exp2(s - m).astype(BF16)
        va = jnp.concatenate([v_ref[0, :, pl.ds(start, 2 * blk)], ones_rows], axis=0)
        o = jnp.dot(va, e, preferred_element_type=F32)
        denom = o[d:d + 1] + jnp.exp2(sink - m)
        y = o[0:d] / denom
        for h in range(g):
            gate = gate_ref[0, h * d:(h + 1) * d, i * blk:(i + 1) * blk]
            o_ref[0, h * d:(h + 1) * d, i * blk:(i + 1) * blk] = (
                y[:, h * blk:(h + 1) * blk] * _silu(gate)).astype(o_ref.dtype)


def _swa_attention(pt, sink_rows, *, blk, nsub):
    bsz, _, s = pt.shape
    d = HEAD_DIM
    rows = SWA_GROUP * d
    wq, wk = SWA_Q_HEADS * d, SWA_KV_HEADS * d
    tq = nsub * blk
    k0, v0, g0 = wq // d, (wq + wk) // d, (wq + 2 * wk) // rows
    blocks = 3 * rows * tq * 2 + 2 * d * s * 2 + 8 * SWA_GROUP * blk * 4
    return pl.pallas_call(
        functools.partial(_swa_attn_kernel, blk=blk, nsub=nsub),
        grid=(bsz, SWA_KV_HEADS, s // tq),
        in_specs=[
            pl.BlockSpec((1, rows, tq), lambda b, h, n: (b, h, n)),
            pl.BlockSpec((1, d, s), lambda b, h, n: (b, k0 + h, 0)),
            pl.BlockSpec((1, d, s), lambda b, h, n: (b, v0 + h, 0)),
            pl.BlockSpec((1, rows, tq), lambda b, h, n: (b, g0 + h, n)),
            pl.BlockSpec((1, 1, SWA_GROUP * blk), lambda b, h, n: (h, 0, 0)),
        ],
        out_specs=pl.BlockSpec((1, rows, tq), lambda b, h, n: (b, h, n)),
        out_shape=jax.ShapeDtypeStruct((bsz, wq, s), BF16),
        compiler_params=pltpu.CompilerParams(
            dimension_semantics=("parallel", "parallel", "arbitrary"),
            vmem_limit_bytes=_vmem_limit(blocks, 0, 8 * nsub * 2 * blk * SWA_GROUP * blk * 4),
        ),
        name="swa_attention",
    )(pt, pt, pt, pt, sink_rows)


def _out_proj_kernel(zt_ref, x_ref, w_ref, fg_ref, o_ref, *, final_norm):
    out = x_ref[...] + lax.dot_general(zt_ref[0], w_ref[...], (((0,), (0,)), ((), ())),
                                       preferred_element_type=F32)
    if final_norm:
        ms = jnp.mean(out * out, axis=-1, keepdims=True)
        out = (out * lax.rsqrt(ms + RMS_EPS)) * fg_ref[...]
    o_ref[...] = out


def _out_proj(zt, x2d, w, fg, *, tm, final_norm):
    bsz, wid, s = zt.shape
    m, d = x2d.shape
    sb = s // tm
    blocks = wid * tm * 2 + 2 * tm * d * 4 + wid * d * 2 + d * 4
    return pl.pallas_call(
        functools.partial(_out_proj_kernel, final_norm=final_norm),
        grid=(m // tm,),
        in_specs=[
            pl.BlockSpec((1, wid, tm), lambda i: (i // sb, 0, i % sb)),
            pl.BlockSpec((tm, d), lambda i: (i, 0)),
            pl.BlockSpec((wid, d), lambda i: (0, 0)),
            pl.BlockSpec((1, d), lambda i: (0, 0)),
        ],
        out_specs=pl.BlockSpec((tm, d), lambda i: (i, 0)),
        out_shape=jax.ShapeDtypeStruct((m, d), F32),
        compiler_params=pltpu.CompilerParams(
            dimension_semantics=("parallel",),
            vmem_limit_bytes=_vmem_limit(blocks, 0, wid * tm * 2 + 3 * tm * d * 4),
        ),
        name="out_proj",
    )(zt, x2d, w, fg)


def _rope_tables(s):
    inv_freq = ROPE_THETA ** (-jnp.arange(ROT_HALF, dtype=F32) / ROT_HALF)
    ang = jnp.arange(s, dtype=F32)[:, None] * inv_freq[None, :]
    return jnp.cos(ang).T, jnp.sin(ang).T


def kernel(x, norm_g, fox_w_in, fox_b_f, fox_w_out, swa_w_in, swa_sinks, swa_w_out, final_g):
    bsz, s, d = x.shape
    x2d = x.reshape(bsz * s, d)

    wid = FOX_HEADS * HEAD_DIM
    wt = jnp.pad(fox_w_in[0], ((0, 0), (0, LANES - FOX_HEADS))).T.astype(BF16)
    pt, zt = _norm_proj(x2d, norm_g[0][None, :], wt, bsz=bsz, n=4 * wid, mode="fox",
                        tm=512, tn=2048, q_blocks=wid // 2048)
    qx, kx = _decay_rows(zt, fox_b_f[0][:, None], FOX_HEADS)
    yt = _fox_attention(pt, qx, kx, n_heads=FOX_HEADS, g_heads=8, tq=512, tk=256)
    x2d = _out_proj(yt, x2d, fox_w_out[0].astype(BF16), final_g[None, :], tm=512, final_norm=False)

    wq = SWA_Q_HEADS * HEAD_DIM
    wt = swa_w_in[0].T.astype(BF16)
    n_swa = wt.shape[0]
    head = jnp.arange(n_swa // HEAD_DIM)
    head_scale = jnp.where(head < SWA_Q_HEADS, QK_SCALE, 1.0).astype(F32)
    head_rope = (head < SWA_Q_HEADS + SWA_KV_HEADS).astype(F32)
    (pt,) = _norm_proj(x2d, norm_g[1][None, :], wt, bsz=bsz, n=n_swa, mode="swa", tm=512, tn=1536,
                       tables=(head_scale, head_rope) + _rope_tables(s))
    sink_rows = jnp.repeat(swa_sinks[0].reshape(SWA_KV_HEADS, SWA_GROUP) * LOG2E, SWA_WINDOW, axis=1)[:, None, :]
    yt = _swa_attention(pt, sink_rows, blk=SWA_WINDOW, nsub=8)
    out = _out_proj(yt, x2d, swa_w_out[0].astype(BF16), final_g[None, :], tm=512, final_norm=True)
    return out.reshape(bsz, s, d)
```

```python
import functools
import math

import jax
import jax.numpy as jnp
from jax import lax
from jax.experimental import pallas as pl
from jax.experimental.pallas import tpu as pltpu

F32 = jnp.float32
BF16 = jnp.bfloat16

RMS_EPS = 1e-6
NEG_INF = -1e30
LOG2E = 1.4426950408889634
HEAD_DIM = 64
FOX_HEADS = 32
SWA_Q_HEADS = 32
SWA_KV_HEADS = 4
SWA_GROUP = SWA_Q_HEADS // SWA_KV_HEADS
SWA_WINDOW = 128
ROPE_THETA = 500000.0
ROT_DIM = HEAD_DIM // 4
ROT_HALF = ROT_DIM // 2
XROWS = 16
LANES = 128
VMEM_CAPACITY = 64 * 1024 * 1024
QK_SCALE = HEAD_DIM ** -0.5 * LOG2E


def _silu(x):
    xf = x.astype(F32)
    return (0.5 * xf) * (1.0 + jnp.tanh(0.5 * xf))


def _vmem_limit(pipelined_block_bytes, scratch_bytes, temp_bytes):
    need = 2 * pipelined_block_bytes + scratch_bytes + temp_bytes
    return int(min(need, VMEM_CAPACITY - (4 << 20)))


_NT = (((1,), (1,)), ((), ()))


def _norm_to_scratch(x_ref, g_ref, h_ref):
    x = x_ref[...]
    ms = jnp.mean(x * x, axis=-1, keepdims=True)
    h_ref[...] = ((x * lax.rsqrt(ms + RMS_EPS)) * g_ref[...]).astype(BF16)


def _norm_proj_fox_kernel(x_ref, g_ref, wt_ref, wft_ref, o_ref, of_ref, h_ref, *, q_blocks):
    j = pl.program_id(1)

    @pl.when(j == 0)
    def _():
        _norm_to_scratch(x_ref, g_ref, h_ref)
        of_ref[0] = lax.dot_general(wft_ref[...], h_ref[...], _NT, preferred_element_type=F32)

    res = lax.dot_general(wt_ref[...], h_ref[...], _NT, preferred_element_type=F32)
    o_ref[0] = (res * jnp.where(j < q_blocks, QK_SCALE, 1.0)).astype(o_ref.dtype)


def _norm_proj_swa_kernel(scale_ref, rope_ref, x_ref, g_ref, wt_ref, cos_ref, sin_ref, o_ref, h_ref):
    j = pl.program_id(1)
    tn = o_ref.shape[1]
    heads = tn // HEAD_DIM

    @pl.when(j == 0)
    def _():
        _norm_to_scratch(x_ref, g_ref, h_ref)

    res = lax.dot_general(wt_ref[...], h_ref[...], _NT, preferred_element_type=F32)
    cos, sin = cos_ref[...], sin_ref[...]
    for hh in range(heads):
        base = hh * HEAD_DIM
        scale = scale_ref[j * heads + hh]
        rope = rope_ref[j * heads + hh]
        ch = (1.0 + rope * (cos - 1.0)) * scale
        sh = (rope * sin) * scale
        x1 = res[base:base + ROT_HALF]
        x2 = res[base + ROT_HALF:base + ROT_DIM]
        o_ref[0, base:base + ROT_DIM, :] = jnp.concatenate(
            [x1 * ch - x2 * sh, x2 * ch + x1 * sh], axis=0).astype(o_ref.dtype)
        o_ref[0, base + ROT_DIM:base + HEAD_DIM, :] = (res[base + ROT_DIM:base + HEAD_DIM] * scale).astype(o_ref.dtype)


def _transpose_cast_kernel(w_ref, o_ref):
    o_ref[...] = w_ref[...].T.astype(o_ref.dtype)


def _transpose_cast(w, n, tn):
    d = w.shape[0]
    return pl.pallas_call(
        _transpose_cast_kernel,
        grid=(n // tn,),
        in_specs=[pl.BlockSpec((d, tn), lambda j: (0, j))],
        out_specs=pl.BlockSpec((tn, d), lambda j: (j, 0)),
        out_shape=jax.ShapeDtypeStruct((n, d), BF16),
        compiler_params=pltpu.CompilerParams(
            dimension_semantics=("parallel",),
            vmem_limit_bytes=_vmem_limit(d * tn * 4 + tn * d * 2, 0, 2 * d * tn * 4),
        ),
        name="transpose_cast",
    )(w)


def _norm_proj(x2d, g, wt, *, bsz, mode, tm, tn, q_blocks=None, wft=None, tables=None):
    m, d = x2d.shape
    n = wt.shape[0]
    s = m // bsz
    sb = s // tm
    in_specs = [
        pl.BlockSpec((tm, d), lambda i, j: (i, 0)),
        pl.BlockSpec((1, d), lambda i, j: (0, 0)),
        pl.BlockSpec((tn, d), lambda i, j: (j, 0)),
    ]
    args = [x2d, g, wt]
    out_specs = [pl.BlockSpec((1, tn, tm), lambda i, j: (i // sb, j, i % sb))]
    out_shape = [jax.ShapeDtypeStruct((bsz, n, s), BF16)]
    blocks = tm * d * 4 + d * 4 + tn * d * 2 + tn * tm * 2
    if mode == "fox":
        body = functools.partial(_norm_proj_fox_kernel, q_blocks=q_blocks)
        in_specs.append(pl.BlockSpec((LANES, d), lambda i, j: (0, 0)))
        args.append(wft)
        out_specs.append(pl.BlockSpec((1, LANES, tm), lambda i, j: (i // sb, 0, i % sb)))
        out_shape.append(jax.ShapeDtypeStruct((bsz, LANES, s), F32))
        blocks += LANES * d * 2 + LANES * tm * 4
    else:
        body = _norm_proj_swa_kernel
        scale, rope, cos, sin = tables
        in_specs = [pl.BlockSpec(memory_space=pltpu.SMEM)] * 2 + in_specs
        in_specs += [pl.BlockSpec((ROT_HALF, tm), lambda i, j: (0, i % sb))] * 2
        args = [scale, rope] + args + [cos, sin]
        blocks += 2 * ROT_HALF * tm * 4
    return pl.pallas_call(
        body,
        grid=(m // tm, n // tn),
        in_specs=in_specs,
        out_specs=out_specs,
        out_shape=out_shape,
        scratch_shapes=[pltpu.VMEM((tm, d), BF16)],
        compiler_params=pltpu.CompilerParams(
            dimension_semantics=("parallel", "arbitrary"),
            vmem_limit_bytes=_vmem_limit(blocks, tm * d * 2, 3 * tm * d * 4 + 2 * tn * tm * 4),
        ),
        name="norm_proj_" + mode,
    )(*args)


def _decay_rows_kernel(z_ref, b_ref, qx_ref, kx_ref):
    z = z_ref[0] + b_ref[...]
    x = (jnp.minimum(z, 0.0) - jnp.log1p(jnp.exp(-jnp.abs(z)))) * LOG2E
    n_heads, s = x.shape
    col = lax.broadcasted_iota(jnp.int32, x.shape, 1)
    shift = 1
    while shift < s:
        x = x + jnp.where(col >= shift, pltpu.roll(x, shift, axis=1), 0.0)
        shift *= 2
    hi = x.astype(BF16).astype(F32)
    mid = (x - hi).astype(BF16).astype(F32)
    lo = (x - hi - mid).astype(BF16).astype(F32)
    row = lax.broadcasted_iota(jnp.int32, (XROWS, s), 0)
    for h in range(n_heads):
        parts_q = jnp.where(row == 0, hi[h:h + 1], jnp.where(row == 1, mid[h:h + 1], lo[h:h + 1]))
        parts_k = jnp.where(row == 3, hi[h:h + 1], jnp.where(row == 4, mid[h:h + 1], lo[h:h + 1]))
        qx = jnp.where(row < 3, parts_q, jnp.where(row < 6, 1.0, 0.0))
        kx = jnp.where(row < 3, 1.0, jnp.where(row < 6, -parts_k, 0.0))
        qx_ref[0, h] = qx.astype(BF16)
        kx_ref[0, h] = kx.astype(BF16)


def _decay_rows(z, b, n_heads):
    bsz, _, s = z.shape
    out_blk = n_heads * XROWS * s * 2
    return pl.pallas_call(
        _decay_rows_kernel,
        grid=(bsz,),
        in_specs=[
            pl.BlockSpec((1, n_heads, s), lambda i: (i, 0, 0)),
            pl.BlockSpec((n_heads, 1), lambda i: (0, 0)),
        ],
        out_specs=[pl.BlockSpec((1, n_heads, XROWS, s), lambda i: (i, 0, 0, 0))] * 2,
        out_shape=[jax.ShapeDtypeStruct((bsz, n_heads, XROWS, s), BF16)] * 2,
        compiler_params=pltpu.CompilerParams(
            dimension_semantics=("parallel",),
            vmem_limit_bytes=_vmem_limit(n_heads * s * 4 + 2 * out_blk, 0, 8 * n_heads * s * 4),
        ),
        name="decay_rows",
    )(z, b)


def _fox_attn_kernel(q_ref, k_ref, v_ref, gate_ref, qx_ref, kx_ref, o_ref,
                     qa_ref, s0_ref, s1_ref, bm0_ref, bm1_ref, m_ref, acc_ref, *, g_heads, tq, tk):
    qi = pl.program_id(2)
    n_diag = tq // tk
    n_full = qi * n_diag
    d = HEAD_DIM
    s_refs = (s0_ref, s1_ref)
    bm_refs = (bm0_ref, bm1_ref)

    for g in range(g_heads):
        qa_ref[g, 0:d, :] = q_ref[0, g * d:(g + 1) * d, :]
        qa_ref[g, d:d + XROWS, :] = qx_ref[0, g]
    m_ref[...] = jnp.full(m_ref.shape, NEG_INF, F32)
    acc_ref[...] = jnp.zeros(acc_ref.shape, F32)

    ones_rows = (lax.broadcasted_iota(jnp.int32, (XROWS, tk), 0) == 0).astype(BF16)

    def scores(j, slot, g, col0=0):
        off = pl.multiple_of(j * tk, tk)
        ka = jnp.concatenate([k_ref[0, g * d:(g + 1) * d, pl.ds(off, tk)],
                              kx_ref[0, g, :, pl.ds(off, tk)]], axis=0)
        s = lax.dot_general(ka, qa_ref[g, :, col0:], (((0,), (0,)), ((), ())),
                            preferred_element_type=F32)
        s_refs[slot][g, :, col0:] = s
        bm_refs[slot][g, :, col0:] = jnp.max(s, axis=0, keepdims=True)

    def softmax_pv(j, slot, g, diag):
        col0 = 0 if diag is None else diag * tk
        off = pl.multiple_of(j * tk, tk)
        s = s_refs[slot][g, :, col0:]
        if diag is None:
            bm = bm_refs[slot][g]
        else:
            krow = lax.broadcasted_iota(jnp.int32, s.shape, 0)
            qcol = lax.broadcasted_iota(jnp.int32, s.shape, 1)
            s = jnp.where(krow <= qcol, s, NEG_INF)
            bm = jnp.max(s, axis=0, keepdims=True)
        m_prev = m_ref[g, :, col0:]
        m_new = jnp.maximum(m_prev, bm)
        alpha = jnp.exp2(m_prev - m_new)
        p = jnp.exp2(s - m_new).astype(BF16)
        va = jnp.concatenate([v_ref[0, g * d:(g + 1) * d, pl.ds(off, tk)], ones_rows], axis=0)
        acc_ref[g, :, col0:] = alpha * acc_ref[g, :, col0:] + jnp.dot(va, p, preferred_element_type=F32)
        m_ref[g, :, col0:] = m_new

    def step(j, slot, diag, next_col0=0):
        for g in range(g_heads):
            if next_col0 is not None:
                scores(j + 1, 1 - slot, g, next_col0)
            softmax_pv(j, slot, g, diag)

    for g in range(g_heads):
        scores(0, 0, g)

    def two_steps(jj, carry):
        step(2 * jj, 0, None)
        step(2 * jj + 1, 1, None)
        return carry

    assert n_diag == 2
    lax.fori_loop(0, n_full // 2, two_steps, 0)
    step(n_full, 0, 0, next_col0=tk)
    step(n_full + 1, 1, 1, next_col0=None)

    for g in range(g_heads):
        y = acc_ref[g, 0:d, :] / acc_ref[g, d:d + 1, :]
        o_ref[0, g * d:(g + 1) * d, :] = (y * _silu(gate_ref[0, g * d:(g + 1) * d, :])).astype(o_ref.dtype)


def _fox_attention(pt, qx, kx, *, n_heads, g_heads, tq, tk):
    bsz, _, s = pt.shape
    d = HEAD_DIM
    rows = g_heads * d
    nw = n_heads // g_heads
    blocks = 3 * rows * tq * 2 + 2 * rows * s * 2 + g_heads * XROWS * (tq + s) * 2
    scratch = g_heads * ((d + XROWS) * tq * 2 + 3 * 8 * tq * 4 + (d + XROWS) * tq * 4 + 2 * tk * tq * 4)
    return pl.pallas_call(
        functools.partial(_fox_attn_kernel, g_heads=g_heads, tq=tq, tk=tk),
        grid=(bsz, nw, s // tq),
        in_specs=[
            pl.BlockSpec((1, rows, tq), lambda b, hg, qi: (b, hg, qi)),
            pl.BlockSpec((1, rows, s), lambda b, hg, qi: (b, nw + hg, 0)),
            pl.BlockSpec((1, rows, s), lambda b, hg, qi: (b, 2 * nw + hg, 0)),
            pl.BlockSpec((1, rows, tq), lambda b, hg, qi: (b, 3 * nw + hg, qi)),
            pl.BlockSpec((1, g_heads, XROWS, tq), lambda b, hg, qi: (b, hg, 0, qi)),
            pl.BlockSpec((1, g_heads, XROWS, s), lambda b, hg, qi: (b, hg, 0, 0)),
        ],
        out_specs=pl.BlockSpec((1, rows, tq), lambda b, hg, qi: (b, hg, qi)),
        out_shape=jax.ShapeDtypeStruct((bsz, n_heads * d, s), BF16),
        scratch_shapes=[
            pltpu.VMEM((g_heads, d + XROWS, tq), BF16),
            pltpu.VMEM((g_heads, tk, tq), F32),
            pltpu.VMEM((g_heads, tk, tq), F32),
            pltpu.VMEM((g_heads, 1, tq), F32),
            pltpu.VMEM((g_heads, 1, tq), F32),
            pltpu.VMEM((g_heads, 1, tq), F32),
            pltpu.VMEM((g_heads, d + XROWS, tq), F32),
        ],
        compiler_params=pltpu.CompilerParams(
            dimension_semantics=("parallel", "parallel", "arbitrary"),
            vmem_limit_bytes=_vmem_limit(blocks, scratch, 4 * g_heads * tq * tk * 4),
        ),
        name="fox_attention",
    )(pt, pt, pt, pt, qx, kx)


def _swa_attn_kernel(q_ref, k_ref, v_ref, gate_ref, sink_ref, o_ref, *, blk, nsub):
    n = pl.program_id(2)
    d = HEAD_DIM
    g = q_ref.shape[1] // d
    ones_rows = (lax.broadcasted_iota(jnp.int32, (XROWS, 2 * blk), 0) == 0).astype(BF16)
    j_loc = lax.broadcasted_iota(jnp.int32, (2 * blk, blk), 0)
    t_loc = lax.broadcasted_iota(jnp.int32, (2 * blk, blk), 1)
    sink = sink_ref[0]

    def window_start(i):
        return pl.multiple_of(jnp.maximum(n * nsub + i - 1, 0) * blk, blk)

    scores = []
    for i in range(nsub):
        qcat = jnp.concatenate([q_ref[0, h * d:(h + 1) * d, i * blk:(i + 1) * blk] for h in range(g)], axis=1)
        scores.append(lax.dot_general(k_ref[0, :, pl.ds(window_start(i), 2 * blk)], qcat,
                                      (((0,), (0,)), ((), ())), preferred_element_type=F32))

    def band_bias(first):
        diff = t_loc - j_loc + (0 if first else blk)
        return jnp.where((diff >= 0) & (diff < SWA_WINDOW), 0.0, NEG_INF)

    bias = band_bias(False)

    for i in range(nsub):
        start = window_start(i)
        bias_i = jnp.where(n == 0, band_bias(True), bias) if i == 0 else bias
        s = jnp.concatenate([scores[i][:, h * blk:(h + 1) * blk] + bias_i for h in range(g)], axis=1)
        m = jnp.maximum(jnp.max(s, axis=0, keepdims=True), sink)
        e = jnp.exp2(s - m).astype(BF16)
        va = jnp.concatenate([v_ref[0, :, pl.ds(start, 2 * blk)], ones_rows], axis=0)
        o = jnp.dot(va, e, preferred_element_type=F32)
        denom = o[d:d + 1] + jnp.exp2(sink - m)
        y = o[0:d] / denom
        for h in range(g):
            gate = gate_ref[0, h * d:(h + 1) * d, i * blk:(i + 1) * blk]
            o_ref[0, h * d:(h + 1) * d, i * blk:(i + 1) * blk] = (
                y[:, h * blk:(h + 1) * blk] * _silu(gate)).astype(o_ref.dtype)


def _swa_attention(pt, sink_rows, *, blk, nsub):
    bsz, _, s = pt.shape
    d = HEAD_DIM
    rows = SWA_GROUP * d
    wq, wk = SWA_Q_HEADS * d, SWA_KV_HEADS * d
    tq = nsub * blk
    k0, v0, g0 = wq // d, (wq + wk) // d, (wq + 2 * wk) // rows
    blocks = 3 * rows * tq * 2 + 2 * d * s * 2 + 8 * SWA_GROUP * blk * 4
    return pl.pallas_call(
        functools.partial(_swa_attn_kernel, blk=blk, nsub=nsub),
        grid=(bsz, SWA_KV_HEADS, s // tq),
        in_specs=[
            pl.BlockSpec((1, rows, tq), lambda b, h, n: (b, h, n)),
            pl.BlockSpec((1, d, s), lambda b, h, n: (b, k0 + h, 0)),
            pl.BlockSpec((1, d, s), lambda b, h, n: (b, v0 + h, 0)),
            pl.BlockSpec((1, rows, tq), lambda b, h, n: (b, g0 + h, n)),
            pl.BlockSpec((1, 1, SWA_GROUP * blk), lambda b, h, n: (h, 0, 0)),
        ],
        out_specs=pl.BlockSpec((1, rows, tq), lambda b, h, n: (b, h, n)),
        out_shape=jax.ShapeDtypeStruct((bsz, wq, s), BF16),
        compiler_params=pltpu.CompilerParams(
            dimension_semantics=("parallel", "parallel", "arbitrary"),
            vmem_limit_bytes=_vmem_limit(blocks, 0, 8 * nsub * 2 * blk * SWA_GROUP * blk * 4),
        ),
        name="swa_attention",
    )(pt, pt, pt, pt, sink_rows)


def _out_proj_kernel(zt_ref, x_ref, w_ref, fg_ref, o_ref, *, final_norm):
    out = x_ref[...] + lax.dot_general(zt_ref[0], w_ref[...], (((0,), (0,)), ((), ())),
                                       preferred_element_type=F32)
    if final_norm:
        ms = jnp.mean(out * out, axis=-1, keepdims=True)
        out = (out * lax.rsqrt(ms + RMS_EPS)) * fg_ref[...]
    o_ref[...] = out


def _out_proj(zt, x2d, w, fg, *, tm, final_norm):
    bsz, wid, s = zt.shape
    m, d = x2d.shape
    sb = s // tm
    blocks = wid * tm * 2 + 2 * tm * d * 4 + wid * d * 2 + d * 4
    return pl.pallas_call(
        functools.partial(_out_proj_kernel, final_norm=final_norm),
        grid=(m // tm,),
        in_specs=[
            pl.BlockSpec((1, wid, tm), lambda i: (i // sb, 0, i % sb)),
            pl.BlockSpec((tm, d), lambda i: (i, 0)),
            pl.BlockSpec((wid, d), lambda i: (0, 0)),
            pl.BlockSpec((1, d), lambda i: (0, 0)),
        ],
        out_specs=pl.BlockSpec((tm, d), lambda i: (i, 0)),
        out_shape=jax.ShapeDtypeStruct((m, d), F32),
        compiler_params=pltpu.CompilerParams(
            dimension_semantics=("parallel",),
            vmem_limit_bytes=_vmem_limit(blocks, 0, wid * tm * 2 + 3 * tm * d * 4),
        ),
        name="out_proj",
    )(zt, x2d, w, fg)


def _rope_tables(s):
    inv_freq = ROPE_THETA ** (-jnp.arange(ROT_HALF, dtype=F32) / ROT_HALF)
    ang = jnp.arange(s, dtype=F32)[:, None] * inv_freq[None, :]
    return jnp.cos(ang).T, jnp.sin(ang).T


PROJ_TM = 1024
FOX_PROJ_TN = 1024
SWA_PROJ_TN = 1536
WT_TN = 256
OUT_TM = 512
FOX_G, FOX_TQ, FOX_TK = 8, 512, 256
SWA_NSUB = 8


def kernel(x, norm_g, fox_w_in, fox_b_f, fox_w_out, swa_w_in, swa_sinks, swa_w_out, final_g):
    bsz, s, d = x.shape
    x2d = x.reshape(bsz * s, d)

    wid = FOX_HEADS * HEAD_DIM
    wt = _transpose_cast(fox_w_in[0], 4 * wid, WT_TN)
    wft = jnp.pad(fox_w_in[0][:, 4 * wid:].T, ((0, LANES - FOX_HEADS), (0, 0))).astype(BF16)
    pt, zt = _norm_proj(x2d, norm_g[0][None, :], wt, bsz=bsz, mode="fox", wft=wft,
                        tm=PROJ_TM, tn=FOX_PROJ_TN, q_blocks=wid // FOX_PROJ_TN)
    qx, kx = _decay_rows(zt, fox_b_f[0][:, None], FOX_HEADS)
    yt = _fox_attention(pt, qx, kx, n_heads=FOX_HEADS, g_heads=FOX_G, tq=FOX_TQ, tk=FOX_TK)
    x2d = _out_proj(yt, x2d, fox_w_out[0].astype(BF16), final_g[None, :], tm=OUT_TM, final_norm=False)

    wq = SWA_Q_HEADS * HEAD_DIM
    n_swa = swa_w_in.shape[2]
    wt = _transpose_cast(swa_w_in[0], n_swa, WT_TN)
    head = jnp.arange(n_swa // HEAD_DIM)
    head_scale = jnp.where(head < SWA_Q_HEADS, QK_SCALE, 1.0).astype(F32)
    head_rope = (head < SWA_Q_HEADS + SWA_KV_HEADS).astype(F32)
    (pt,) = _norm_proj(x2d, norm_g[1][None, :], wt, bsz=bsz, mode="swa", tm=PROJ_TM, tn=SWA_PROJ_TN,
                       tables=(head_scale, head_rope) + _rope_tables(s))
    sink_rows = jnp.repeat(swa_sinks[0].reshape(SWA_KV_HEADS, SWA_GROUP) * LOG2E, SWA_WINDOW, axis=1)[:, None, :]
    yt = _swa_attention(pt, sink_rows, blk=SWA_WINDOW, nsub=SWA_NSUB)
    out = _out_proj(yt, x2d, swa_w_out[0].astype(BF16), final_g[None, :], tm=OUT_TM, final_norm=True)
    return out.reshape(bsz, s, d)
```

```python
import functools
import math

import jax
import jax.numpy as jnp
from jax import lax
from jax.experimental import pallas as pl
from jax.experimental.pallas import tpu as pltpu

F32 = jnp.float32
BF16 = jnp.bfloat16

RMS_EPS = 1e-6
NEG_INF = -1e30
LOG2E = 1.4426950408889634
HEAD_DIM = 64
FOX_HEADS = 32
SWA_Q_HEADS = 32
SWA_KV_HEADS = 4
SWA_GROUP = SWA_Q_HEADS // SWA_KV_HEADS
SWA_WINDOW = 128
ROPE_THETA = 500000.0
ROT_DIM = HEAD_DIM // 4
ROT_HALF = ROT_DIM // 2
XROWS = 16
LANES = 128
VMEM_CAPACITY = 64 * 1024 * 1024
QK_SCALE = HEAD_DIM ** -0.5 * LOG2E


def _silu(x):
    xf = x.astype(F32)
    return (0.5 * xf) * (1.0 + jnp.tanh(0.5 * xf))


def _vmem_limit(pipelined_block_bytes, scratch_bytes, temp_bytes):
    need = 2 * pipelined_block_bytes + scratch_bytes + temp_bytes
    return int(min(need, VMEM_CAPACITY - (4 << 20)))


_NT = (((1,), (1,)), ((), ()))


def _norm_to_scratch(x_ref, g_ref, h_ref):
    x = x_ref[...]
    ms = jnp.mean(x * x, axis=-1, keepdims=True)
    h_ref[...] = ((x * lax.rsqrt(ms + RMS_EPS)) * g_ref[...]).astype(BF16)


def _norm_proj_fox_kernel(x_ref, g_ref, wt_ref, wft_ref, o_ref, of_ref, h_ref, *, q_blocks):
    j = pl.program_id(1)

    @pl.when(j == 0)
    def _():
        _norm_to_scratch(x_ref, g_ref, h_ref)
        of_ref[0] = lax.dot_general(wft_ref[...], h_ref[...], _NT, preferred_element_type=F32)

    res = lax.dot_general(wt_ref[...], h_ref[...], _NT, preferred_element_type=F32)
    o_ref[0] = (res * jnp.where(j < q_blocks, QK_SCALE, 1.0)).astype(o_ref.dtype)


def _norm_proj_swa_kernel(scale_ref, rope_ref, x_ref, g_ref, wt_ref, cos_ref, sin_ref, o_ref, h_ref):
    j = pl.program_id(1)
    tn = o_ref.shape[1]
    heads = tn // HEAD_DIM

    @pl.when(j == 0)
    def _():
        _norm_to_scratch(x_ref, g_ref, h_ref)

    res = lax.dot_general(wt_ref[...], h_ref[...], _NT, preferred_element_type=F32)
    cos, sin = cos_ref[...], sin_ref[...]
    for hh in range(heads):
        base = hh * HEAD_DIM
        scale = scale_ref[j * heads + hh]
        rope = rope_ref[j * heads + hh]
        ch = (1.0 + rope * (cos - 1.0)) * scale
        sh = (rope * sin) * scale
        x1 = res[base:base + ROT_HALF]
        x2 = res[base + ROT_HALF:base + ROT_DIM]
        o_ref[0, base:base + ROT_DIM, :] = jnp.concatenate(
            [x1 * ch - x2 * sh, x2 * ch + x1 * sh], axis=0).astype(o_ref.dtype)
        o_ref[0, base + ROT_DIM:base + HEAD_DIM, :] = (res[base + ROT_DIM:base + HEAD_DIM] * scale).astype(o_ref.dtype)


def _transpose_cast_kernel(w_ref, o_ref):
    o_ref[...] = w_ref[...].T.astype(o_ref.dtype)


def _transpose_cast(w, n, tn):
    d = w.shape[0]
    return pl.pallas_call(
        _transpose_cast_kernel,
        grid=(n // tn,),
        in_specs=[pl.BlockSpec((d, tn), lambda j: (0, j))],
        out_specs=pl.BlockSpec((tn, d), lambda j: (j, 0)),
        out_shape=jax.ShapeDtypeStruct((n, d), BF16),
        compiler_params=pltpu.CompilerParams(
            dimension_semantics=("parallel",),
            vmem_limit_bytes=_vmem_limit(d * tn * 4 + tn * d * 2, 0, 2 * d * tn * 4),
        ),
        name="transpose_cast",
    )(w)


def _norm_proj(x2d, g, wt, *, bsz, mode, tm, tn, q_blocks=None, n_forget=0, tables=None):
    m, d = x2d.shape
    n = wt.shape[0] - n_forget
    s = m // bsz
    sb = s // tm
    in_specs = [
        pl.BlockSpec((tm, d), lambda i, j: (i, 0)),
        pl.BlockSpec((1, d), lambda i, j: (0, 0)),
        pl.BlockSpec((tn, d), lambda i, j: (j, 0)),
    ]
    args = [x2d, g, wt]
    out_specs = [pl.BlockSpec((1, tn, tm), lambda i, j: (i // sb, j, i % sb))]
    out_shape = [jax.ShapeDtypeStruct((bsz, n, s), BF16)]
    blocks = tm * d * 4 + d * 4 + tn * d * 2 + tn * tm * 2
    if mode == "fox":
        body = functools.partial(_norm_proj_fox_kernel, q_blocks=q_blocks)
        in_specs.append(pl.BlockSpec((n_forget, d), lambda i, j: (n // n_forget, 0)))
        args.append(wt)
        out_specs.append(pl.BlockSpec((1, n_forget, tm), lambda i, j: (i // sb, 0, i % sb)))
        out_shape.append(jax.ShapeDtypeStruct((bsz, n_forget, s), F32))
        blocks += n_forget * d * 2 + n_forget * tm * 4
    else:
        body = _norm_proj_swa_kernel
        scale, rope, cos, sin = tables
        in_specs = [pl.BlockSpec(memory_space=pltpu.SMEM)] * 2 + in_specs
        in_specs += [pl.BlockSpec((ROT_HALF, tm), lambda i, j: (0, i % sb))] * 2
        args = [scale, rope] + args + [cos, sin]
        blocks += 2 * ROT_HALF * tm * 4
    return pl.pallas_call(
        body,
        grid=(m // tm, n // tn),
        in_specs=in_specs,
        out_specs=out_specs,
        out_shape=out_shape,
        scratch_shapes=[pltpu.VMEM((tm, d), BF16)],
        compiler_params=pltpu.CompilerParams(
            dimension_semantics=("parallel", "arbitrary"),
            vmem_limit_bytes=_vmem_limit(blocks, tm * d * 2, 3 * tm * d * 4 + 2 * tn * tm * 4),
        ),
        name="norm_proj_" + mode,
    )(*args)


def _decay_rows_kernel(z_ref, b_ref, qx_ref, kx_ref):
    z = z_ref[0] + b_ref[...]
    x = (jnp.minimum(z, 0.0) - jnp.log1p(jnp.exp(-jnp.abs(z)))) * LOG2E
    n_heads, s = x.shape
    col = lax.broadcasted_iota(jnp.int32, x.shape, 1)
    shift = 1
    while shift < s:
        x = x + jnp.where(col >= shift, pltpu.roll(x, shift, axis=1), 0.0)
        shift *= 2
    hi = x.astype(BF16).astype(F32)
    mid = (x - hi).astype(BF16).astype(F32)
    lo = (x - hi - mid).astype(BF16).astype(F32)
    row = lax.broadcasted_iota(jnp.int32, (XROWS, s), 0)
    for h in range(n_heads):
        parts_q = jnp.where(row == 0, hi[h:h + 1], jnp.where(row == 1, mid[h:h + 1], lo[h:h + 1]))
        parts_k = jnp.where(row == 3, hi[h:h + 1], jnp.where(row == 4, mid[h:h + 1], lo[h:h + 1]))
        qx = jnp.where(row < 3, parts_q, jnp.where(row < 6, 1.0, 0.0))
        kx = jnp.where(row < 3, 1.0, jnp.where(row < 6, -parts_k, 0.0))
        qx_ref[0, h] = qx.astype(BF16)
        kx_ref[0, h] = kx.astype(BF16)


def _decay_rows(z, b, n_heads):
    bsz, _, s = z.shape
    out_blk = n_heads * XROWS * s * 2
    return pl.pallas_call(
        _decay_rows_kernel,
        grid=(bsz,),
        in_specs=[
            pl.BlockSpec((1, n_heads, s), lambda i: (i, 0, 0)),
            pl.BlockSpec((n_heads, 1), lambda i: (0, 0)),
        ],
        out_specs=[pl.BlockSpec((1, n_heads, XROWS, s), lambda i: (i, 0, 0, 0))] * 2,
        out_shape=[jax.ShapeDtypeStruct((bsz, n_heads, XROWS, s), BF16)] * 2,
        compiler_params=pltpu.CompilerParams(
            dimension_semantics=("parallel",),
            vmem_limit_bytes=_vmem_limit(n_heads * s * 4 + 2 * out_blk, 0, 8 * n_heads * s * 4),
        ),
        name="decay_rows",
    )(z, b)


def _fox_attn_kernel(q_ref, k_ref, v_ref, gate_ref, qx_ref, kx_ref, o_ref,
                     qa_ref, s0_ref, s1_ref, bm0_ref, bm1_ref, m_ref, acc_ref, *, g_heads, tq, tk):
    qi = pl.program_id(2)
    n_diag = tq // tk
    n_full = qi * n_diag
    d = HEAD_DIM
    s_refs = (s0_ref, s1_ref)
    bm_refs = (bm0_ref, bm1_ref)

    for g in range(g_heads):
        qa_ref[g, 0:d, :] = q_ref[0, g * d:(g + 1) * d, :]
        qa_ref[g, d:d + XROWS, :] = qx_ref[0, g]
    m_ref[...] = jnp.full(m_ref.shape, NEG_INF, F32)
    acc_ref[...] = jnp.zeros(acc_ref.shape, F32)

    ones_rows = (lax.broadcasted_iota(jnp.int32, (XROWS, tk), 0) == 0).astype(BF16)

    def scores(j, slot, g, col0=0):
        off = pl.multiple_of(j * tk, tk)
        ka = jnp.concatenate([k_ref[0, g * d:(g + 1) * d, pl.ds(off, tk)],
                              kx_ref[0, g, :, pl.ds(off, tk)]], axis=0)
        s = lax.dot_general(ka, qa_ref[g, :, col0:], (((0,), (0,)), ((), ())),
                            preferred_element_type=F32)
        s_refs[slot][g, :, col0:] = s
        bm_refs[slot][g, :, col0:] = jnp.max(s, axis=0, keepdims=True)

    def softmax_pv(j, slot, g, diag):
        col0 = 0 if diag is None else diag * tk
        off = pl.multiple_of(j * tk, tk)
        s = s_refs[slot][g, :, col0:]
        if diag is None:
            bm = bm_refs[slot][g]
        else:
            krow = lax.broadcasted_iota(jnp.int32, s.shape, 0)
            qcol = lax.broadcasted_iota(jnp.int32, s.shape, 1)
            s = jnp.where(krow <= qcol, s, NEG_INF)
            bm = jnp.max(s, axis=0, keepdims=True)
        m_prev = m_ref[g, :, col0:]
        m_new = jnp.maximum(m_prev, bm)
        alpha = jnp.exp2(m_prev - m_new)
        p = jnp.exp2(s - m_new).astype(BF16)
        va = jnp.concatenate([v_ref[0, g * d:(g + 1) * d, pl.ds(off, tk)], ones_rows], axis=0)
        acc_ref[g, :, col0:] = alpha * acc_ref[g, :, col0:] + jnp.dot(va, p, preferred_element_type=F32)
        m_ref[g, :, col0:] = m_new

    def step(j, slot, diag, next_col0=0):
        for g in range(g_heads):
            if next_col0 is not None:
                scores(j + 1, 1 - slot, g, next_col0)
            softmax_pv(j, slot, g, diag)

    for g in range(g_heads):
        scores(0, 0, g)

    def two_steps(jj, carry):
        step(2 * jj, 0, None)
        step(2 * jj + 1, 1, None)
        return carry

    assert n_diag == 2
    lax.fori_loop(0, n_full // 2, two_steps, 0)
    step(n_full, 0, 0, next_col0=tk)
    step(n_full + 1, 1, 1, next_col0=None)

    for g in range(g_heads):
        y = acc_ref[g, 0:d, :] / acc_ref[g, d:d + 1, :]
        o_ref[0, g * d:(g + 1) * d, :] = (y * _silu(gate_ref[0, g * d:(g + 1) * d, :])).astype(o_ref.dtype)


def _fox_attention(pt, qx, kx, *, n_heads, g_heads, tq, tk):
    bsz, _, s = pt.shape
    d = HEAD_DIM
    rows = g_heads * d
    nw = n_heads // g_heads
    blocks = 3 * rows * tq * 2 + 2 * rows * s * 2 + g_heads * XROWS * (tq + s) * 2
    scratch = g_heads * ((d + XROWS) * tq * 2 + 3 * 8 * tq * 4 + (d + XROWS) * tq * 4 + 2 * tk * tq * 4)
    return pl.pallas_call(
        functools.partial(_fox_attn_kernel, g_heads=g_heads, tq=tq, tk=tk),
        grid=(bsz, nw, s // tq),
        in_specs=[
            pl.BlockSpec((1, rows, tq), lambda b, hg, qi: (b, hg, qi)),
            pl.BlockSpec((1, rows, s), lambda b, hg, qi: (b, nw + hg, 0)),
            pl.BlockSpec((1, rows, s), lambda b, hg, qi: (b, 2 * nw + hg, 0)),
            pl.BlockSpec((1, rows, tq), lambda b, hg, qi: (b, 3 * nw + hg, qi)),
            pl.BlockSpec((1, g_heads, XROWS, tq), lambda b, hg, qi: (b, hg, 0, qi)),
            pl.BlockSpec((1, g_heads, XROWS, s), lambda b, hg, qi: (b, hg, 0, 0)),
        ],
        out_specs=pl.BlockSpec((1, rows, tq), lambda b, hg, qi: (b, hg, qi)),
        out_shape=jax.ShapeDtypeStruct((bsz, n_heads * d, s), BF16),
        scratch_shapes=[
            pltpu.VMEM((g_heads, d + XROWS, tq), BF16),
            pltpu.VMEM((g_heads, tk, tq), F32),
            pltpu.VMEM((g_heads, tk, tq), F32),
            pltpu.VMEM((g_heads, 1, tq), F32),
            pltpu.VMEM((g_heads, 1, tq), F32),
            pltpu.VMEM((g_heads, 1, tq), F32),
            pltpu.VMEM((g_heads, d + XROWS, tq), F32),
        ],
        compiler_params=pltpu.CompilerParams(
            dimension_semantics=("parallel", "parallel", "arbitrary"),
            vmem_limit_bytes=_vmem_limit(blocks, scratch, 4 * g_heads * tq * tk * 4),
        ),
        name="fox_attention",
    )(pt, pt, pt, pt, qx, kx)


def _swa_attn_kernel(q_ref, k_ref, v_ref, gate_ref, sink_ref, mask_ref, pick_ref, o_ref, *, blk, nsub):
    n = pl.program_id(2)
    d = HEAD_DIM
    g = q_ref.shape[1] // d
    ones_rows = (lax.broadcasted_iota(jnp.int32, (XROWS, 2 * blk), 0) == 0).astype(BF16)
    sink = sink_ref[0]

    def window_start(i):
        return pl.multiple_of(jnp.maximum(n * nsub + i - 1, 0) * blk, blk)

    scores = []
    for i in range(nsub):
        mask = jnp.where(n == 0, mask_ref[0], mask_ref[1]) if i == 0 else mask_ref[1]
        ka = jnp.concatenate([k_ref[0, :, pl.ds(window_start(i), 2 * blk)], mask], axis=0)
        qa = jnp.concatenate([q_ref[0, h * d:(h + 1) * d, i * blk:(i + 1) * blk] for h in range(g)], axis=1)
        qa = jnp.concatenate([qa, pick_ref[...]], axis=0)
        scores.append(lax.dot_general(ka, qa, (((0,), (0,)), ((), ())), preferred_element_type=F32))

    for i in range(nsub):
        start = window_start(i)
        s = scores[i]
        m = jnp.maximum(jnp.max(s, axis=0, keepdims=True), sink)
        e = jnp.exp2(s - m).astype(BF16)
        va = jnp.concatenate([v_ref[0, :, pl.ds(start, 2 * blk)], ones_rows], axis=0)
        o = jnp.dot(va, e, preferred_element_type=F32)
        denom = o[d:d + 1] + jnp.exp2(sink - m)
        y = o[0:d] / denom
        for h in range(g):
            gate = gate_ref[0, h * d:(h + 1) * d, i * blk:(i + 1) * blk]
            o_ref[0, h * d:(h + 1) * d, i * blk:(i + 1) * blk] = (
                y[:, h * blk:(h + 1) * blk] * _silu(gate)).astype(o_ref.dtype)


def _band_tables(blk):
    t = jnp.arange(blk)[:, None]
    j = jnp.arange(2 * blk)[None, :]

    def mask(first):
        diff = t - j + (0 if first else blk)
        return jnp.where((diff >= 0) & (diff < SWA_WINDOW), 0.0, NEG_INF)

    mask_t = jnp.stack([mask(True), mask(False)]).astype(BF16)
    pick = jnp.tile(jnp.eye(blk, dtype=BF16), (1, SWA_GROUP))
    return mask_t, pick


def _swa_attention(pt, sink_rows, *, blk, nsub):
    bsz, _, s = pt.shape
    d = HEAD_DIM
    rows = SWA_GROUP * d
    wq, wk = SWA_Q_HEADS * d, SWA_KV_HEADS * d
    tq = nsub * blk
    k0, v0, g0 = wq // d, (wq + wk) // d, (wq + 2 * wk) // rows
    mask_t, pick = _band_tables(blk)
    blocks = 3 * rows * tq * 2 + 2 * d * s * 2 + 8 * SWA_GROUP * blk * 4 + 4 * blk * blk * 2 + blk * rows * 2
    return pl.pallas_call(
        functools.partial(_swa_attn_kernel, blk=blk, nsub=nsub),
        grid=(bsz, SWA_KV_HEADS, s // tq),
        in_specs=[
            pl.BlockSpec((1, rows, tq), lambda b, h, n: (b, h, n)),
            pl.BlockSpec((1, d, s), lambda b, h, n: (b, k0 + h, 0)),
            pl.BlockSpec((1, d, s), lambda b, h, n: (b, v0 + h, 0)),
            pl.BlockSpec((1, rows, tq), lambda b, h, n: (b, g0 + h, n)),
            pl.BlockSpec((1, 1, SWA_GROUP * blk), lambda b, h, n: (h, 0, 0)),
            pl.BlockSpec((2, blk, 2 * blk), lambda b, h, n: (0, 0, 0)),
            pl.BlockSpec((blk, SWA_GROUP * blk), lambda b, h, n: (0, 0)),
        ],
        out_specs=pl.BlockSpec((1, rows, tq), lambda b, h, n: (b, h, n)),
        out_shape=jax.ShapeDtypeStruct((bsz, wq, s), BF16),
        compiler_params=pltpu.CompilerParams(
            dimension_semantics=("parallel", "parallel", "arbitrary"),
            vmem_limit_bytes=_vmem_limit(blocks, 0, 8 * nsub * 2 * blk * SWA_GROUP * blk * 4),
        ),
        name="swa_attention",
    )(pt, pt, pt, pt, sink_rows, mask_t, pick)


def _out_proj_kernel(zt_ref, x_ref, w_ref, fg_ref, o_ref, *, final_norm):
    out = x_ref[...] + lax.dot_general(zt_ref[0], w_ref[...], (((0,), (0,)), ((), ())),
                                       preferred_element_type=F32)
    if final_norm:
        ms = jnp.mean(out * out, axis=-1, keepdims=True)
        out = (out * lax.rsqrt(ms + RMS_EPS)) * fg_ref[...]
    o_ref[...] = out


def _out_proj(zt, x2d, w, fg, *, tm, final_norm):
    bsz, wid, s = zt.shape
    m, d = x2d.shape
    sb = s // tm
    blocks = wid * tm * 2 + 2 * tm * d * 4 + wid * d * 2 + d * 4
    return pl.pallas_call(
        functools.partial(_out_proj_kernel, final_norm=final_norm),
        grid=(m // tm,),
        in_specs=[
            pl.BlockSpec((1, wid, tm), lambda i: (i // sb, 0, i % sb)),
            pl.BlockSpec((tm, d), lambda i: (i, 0)),
            pl.BlockSpec((wid, d), lambda i: (0, 0)),
            pl.BlockSpec((1, d), lambda i: (0, 0)),
        ],
        out_specs=pl.BlockSpec((tm, d), lambda i: (i, 0)),
        out_shape=jax.ShapeDtypeStruct((m, d), F32),
        compiler_params=pltpu.CompilerParams(
            dimension_semantics=("parallel",),
            vmem_limit_bytes=_vmem_limit(blocks, 0, wid * tm * 2 + 3 * tm * d * 4),
        ),
        name="out_proj",
    )(zt, x2d, w, fg)


def _rope_tables(s):
    inv_freq = ROPE_THETA ** (-jnp.arange(ROT_HALF, dtype=F32) / ROT_HALF)
    ang = jnp.arange(s, dtype=F32)[:, None] * inv_freq[None, :]
    return jnp.cos(ang).T, jnp.sin(ang).T


PROJ_TM = 1024
FOX_PROJ_TN = 2048
SWA_PROJ_TN = 1536
WT_TN = 256
OUT_TM = 512
FOX_G, FOX_TQ, FOX_TK = 8, 512, 256
SWA_NSUB = 8


def kernel(x, norm_g, fox_w_in, fox_b_f, fox_w_out, swa_w_in, swa_sinks, swa_w_out, final_g):
    bsz, s, d = x.shape
    x2d = x.reshape(bsz * s, d)

    wid = FOX_HEADS * HEAD_DIM
    wt = fox_w_in[0].T.astype(BF16)
    pt, zt = _norm_proj(x2d, norm_g[0][None, :], wt, bsz=bsz, mode="fox", n_forget=FOX_HEADS,
                        tm=PROJ_TM, tn=FOX_PROJ_TN, q_blocks=wid // FOX_PROJ_TN)
    qx, kx = _decay_rows(zt, fox_b_f[0][:, None], FOX_HEADS)
    yt = _fox_attention(pt, qx, kx, n_heads=FOX_HEADS, g_heads=FOX_G, tq=FOX_TQ, tk=FOX_TK)
    x2d = _out_proj(yt, x2d, fox_w_out[0].astype(BF16), final_g[None, :], tm=OUT_TM, final_norm=False)

    wq = SWA_Q_HEADS * HEAD_DIM
    n_swa = swa_w_in.shape[2]
    wt = _transpose_cast(swa_w_in[0], n_swa, WT_TN)
    head = jnp.arange(n_swa // HEAD_DIM)
    head_scale = jnp.where(head < SWA_Q_HEADS, QK_SCALE, 1.0).astype(F32)
    head_rope = (head < SWA_Q_HEADS + SWA_KV_HEADS).astype(F32)
    (pt,) = _norm_proj(x2d, norm_g[1][None, :], wt, bsz=bsz, mode="swa", tm=PROJ_TM, tn=SWA_PROJ_TN,
                       tables=(head_scale, head_rope) + _rope_tables(s))
    sink_rows = jnp.repeat(swa_sinks[0].reshape(SWA_KV_HEADS, SWA_GROUP) * LOG2E, SWA_WINDOW, axis=1)[:, None, :]
    yt = _swa_attention(pt, sink_rows, blk=SWA_WINDOW, nsub=SWA_NSUB)
    out = _out_proj(yt, x2d, swa_w_out[0].astype(BF16), final_g[None, :], tm=OUT_TM, final_norm=True)
    return out.reshape(bsz, s, d)
```

```python
import functools
import math

import jax
import jax.numpy as jnp
from jax import lax
from jax.experimental import pallas as pl
from jax.experimental.pallas import tpu as pltpu

F32 = jnp.float32
BF16 = jnp.bfloat16

RMS_EPS = 1e-6
NEG_INF = -1e30
LOG2E = 1.4426950408889634
HEAD_DIM = 64
FOX_HEADS = 32
SWA_Q_HEADS = 32
SWA_KV_HEADS = 4
SWA_GROUP = SWA_Q_HEADS // SWA_KV_HEADS
SWA_WINDOW = 128
ROPE_THETA = 500000.0
ROT_DIM = HEAD_DIM // 4
ROT_HALF = ROT_DIM // 2
XROWS = 16
LANES = 128
VMEM_CAPACITY = 64 * 1024 * 1024
QK_SCALE = HEAD_DIM ** -0.5 * LOG2E


def _gated(y, gate):
    half = 0.5 * gate.astype(F32)
    return (y * (half * (1.0 + jnp.tanh(half)))).astype(gate.dtype)


def _vmem_limit(pipelined_block_bytes, scratch_bytes, temp_bytes):
    need = 2 * pipelined_block_bytes + scratch_bytes + temp_bytes
    return int(min(need, VMEM_CAPACITY - (4 << 20)))


_NT = (((1,), (1,)), ((), ()))


def _norm_to_scratch(x_ref, g_ref, h_ref):
    x = x_ref[...]
    ms = jnp.mean(x * x, axis=-1, keepdims=True)
    h_ref[...] = ((x * lax.rsqrt(ms + RMS_EPS)) * g_ref[...]).astype(BF16)


def _norm_proj_fox_kernel(x_ref, g_ref, wt_ref, wft_ref, o_ref, of_ref, h_ref, *, q_blocks):
    j = pl.program_id(1)

    @pl.when(j == 0)
    def _():
        _norm_to_scratch(x_ref, g_ref, h_ref)
        of_ref[0] = lax.dot_general(wft_ref[...], h_ref[...], _NT, preferred_element_type=F32)

    res = lax.dot_general(wt_ref[...], h_ref[...], _NT, preferred_element_type=F32)
    o_ref[0] = (res * jnp.where(j < q_blocks, QK_SCALE, 1.0)).astype(o_ref.dtype)


def _norm_proj_swa_kernel(scale_ref, rope_ref, x_ref, g_ref, wt_ref, cos_ref, sin_ref, o_ref, h_ref):
    j = pl.program_id(1)
    tn = o_ref.shape[1]
    heads = tn // HEAD_DIM

    @pl.when(j == 0)
    def _():
        _norm_to_scratch(x_ref, g_ref, h_ref)

    res = lax.dot_general(wt_ref[...], h_ref[...], _NT, preferred_element_type=F32)
    cos, sin = cos_ref[...], sin_ref[...]
    for hh in range(heads):
        base = hh * HEAD_DIM
        scale = scale_ref[j * heads + hh]
        rope = rope_ref[j * heads + hh]
        ch = (1.0 + rope * (cos - 1.0)) * scale
        sh = (rope * sin) * scale
        x1 = res[base:base + ROT_HALF]
        x2 = res[base + ROT_HALF:base + ROT_DIM]
        o_ref[0, base:base + ROT_DIM, :] = jnp.concatenate(
            [x1 * ch - x2 * sh, x2 * ch + x1 * sh], axis=0).astype(o_ref.dtype)
        o_ref[0, base + ROT_DIM:base + HEAD_DIM, :] = (res[base + ROT_DIM:base + HEAD_DIM] * scale).astype(o_ref.dtype)


def _transpose_cast_kernel(w_ref, o_ref):
    o_ref[...] = w_ref[...].T.astype(o_ref.dtype)


def _transpose_cast(w, n, tn):
    d = w.shape[0]
    return pl.pallas_call(
        _transpose_cast_kernel,
        grid=(n // tn,),
        in_specs=[pl.BlockSpec((d, tn), lambda j: (0, j))],
        out_specs=pl.BlockSpec((tn, d), lambda j: (j, 0)),
        out_shape=jax.ShapeDtypeStruct((n, d), BF16),
        compiler_params=pltpu.CompilerParams(
            dimension_semantics=("parallel",),
            vmem_limit_bytes=_vmem_limit(d * tn * 4 + tn * d * 2, 0, 2 * d * tn * 4),
        ),
        name="transpose_cast",
    )(w)


def _norm_proj(x2d, g, wt, *, bsz, mode, tm, tn, q_blocks=None, n_forget=0, tables=None):
    m, d = x2d.shape
    n = wt.shape[0] - n_forget
    s = m // bsz
    sb = s // tm
    in_specs = [
        pl.BlockSpec((tm, d), lambda i, j: (i, 0)),
        pl.BlockSpec((1, d), lambda i, j: (0, 0)),
        pl.BlockSpec((tn, d), lambda i, j: (j, 0)),
    ]
    args = [x2d, g, wt]
    out_specs = [pl.BlockSpec((1, tn, tm), lambda i, j: (i // sb, j, i % sb))]
    out_shape = [jax.ShapeDtypeStruct((bsz, n, s), BF16)]
    blocks = tm * d * 4 + d * 4 + tn * d * 2 + tn * tm * 2
    if mode == "fox":
        body = functools.partial(_norm_proj_fox_kernel, q_blocks=q_blocks)
        in_specs.append(pl.BlockSpec((n_forget, d), lambda i, j: (n // n_forget, 0)))
        args.append(wt)
        out_specs.append(pl.BlockSpec((1, n_forget, tm), lambda i, j: (i // sb, 0, i % sb)))
        out_shape.append(jax.ShapeDtypeStruct((bsz, n_forget, s), F32))
        blocks += n_forget * d * 2 + n_forget * tm * 4
    else:
        body = _norm_proj_swa_kernel
        scale, rope, cos, sin = tables
        in_specs = [pl.BlockSpec(memory_space=pltpu.SMEM)] * 2 + in_specs
        in_specs += [pl.BlockSpec((ROT_HALF, tm), lambda i, j: (0, i % sb))] * 2
        args = [scale, rope] + args + [cos, sin]
        blocks += 2 * ROT_HALF * tm * 4
    return pl.pallas_call(
        body,
        grid=(m // tm, n // tn),
        in_specs=in_specs,
        out_specs=out_specs,
        out_shape=out_shape,
        scratch_shapes=[pltpu.VMEM((tm, d), BF16)],
        compiler_params=pltpu.CompilerParams(
            dimension_semantics=("parallel", "arbitrary"),
            vmem_limit_bytes=_vmem_limit(blocks, tm * d * 2, 3 * tm * d * 4 + 2 * tn * tm * 4),
        ),
        name="norm_proj_" + mode,
    )(*args)


def _decay_rows_kernel(z_ref, b_ref, qx_ref, kx_ref):
    z = z_ref[0] + b_ref[...]
    x = (jnp.minimum(z, 0.0) - jnp.log1p(jnp.exp(-jnp.abs(z)))) * LOG2E
    n_heads, s = x.shape
    col = lax.broadcasted_iota(jnp.int32, x.shape, 1)
    shift = 1
    while shift < s:
        x = x + jnp.where(col >= shift, pltpu.roll(x, shift, axis=1), 0.0)
        shift *= 2
    hi = x.astype(BF16).astype(F32)
    mid = (x - hi).astype(BF16).astype(F32)
    lo = (x - hi - mid).astype(BF16).astype(F32)
    row = lax.broadcasted_iota(jnp.int32, (XROWS, s), 0)
    for h in range(n_heads):
        parts_q = jnp.where(row == 0, hi[h:h + 1], jnp.where(row == 1, mid[h:h + 1], lo[h:h + 1]))
        parts_k = jnp.where(row == 3, hi[h:h + 1], jnp.where(row == 4, mid[h:h + 1], lo[h:h + 1]))
        qx = jnp.where(row < 3, parts_q, jnp.where(row < 6, 1.0, 0.0))
        kx = jnp.where(row < 3, 1.0, jnp.where(row < 6, -parts_k, 0.0))
        qx_ref[0, h] = qx.astype(BF16)
        kx_ref[0, h] = kx.astype(BF16)


def _decay_rows(z, b, n_heads):
    bsz, _, s = z.shape
    out_blk = n_heads * XROWS * s * 2
    return pl.pallas_call(
        _decay_rows_kernel,
        grid=(bsz,),
        in_specs=[
            pl.BlockSpec((1, n_heads, s), lambda i: (i, 0, 0)),
            pl.BlockSpec((n_heads, 1), lambda i: (0, 0)),
        ],
        out_specs=[pl.BlockSpec((1, n_heads, XROWS, s), lambda i: (i, 0, 0, 0))] * 2,
        out_shape=[jax.ShapeDtypeStruct((bsz, n_heads, XROWS, s), BF16)] * 2,
        compiler_params=pltpu.CompilerParams(
            dimension_semantics=("parallel",),
            vmem_limit_bytes=_vmem_limit(n_heads * s * 4 + 2 * out_blk, 0, 8 * n_heads * s * 4),
        ),
        name="decay_rows",
    )(z, b)


def _fox_attn_kernel(q_ref, k_ref, v_ref, gate_ref, qx_ref, kx_ref, o_ref,
                     qa_ref, s0_ref, s1_ref, bm0_ref, bm1_ref, m_ref, acc_ref, *, g_heads, tq, tk):
    qi = pl.program_id(2)
    n_diag = tq // tk
    n_full = qi * n_diag
    d = HEAD_DIM
    s_refs = (s0_ref, s1_ref)
    bm_refs = (bm0_ref, bm1_ref)

    for g in range(g_heads):
        qa_ref[g, 0:d, :] = q_ref[0, g * d:(g + 1) * d, :]
        qa_ref[g, d:d + XROWS, :] = qx_ref[0, g]
    m_ref[...] = jnp.full(m_ref.shape, NEG_INF, F32)
    acc_ref[...] = jnp.zeros(acc_ref.shape, F32)

    ones_rows = (lax.broadcasted_iota(jnp.int32, (XROWS, tk), 0) == 0).astype(BF16)

    def scores(j, slot, g, col0=0):
        off = pl.multiple_of(j * tk, tk)
        ka = jnp.concatenate([k_ref[0, g * d:(g + 1) * d, pl.ds(off, tk)],
                              kx_ref[0, g, :, pl.ds(off, tk)]], axis=0)
        s = lax.dot_general(ka, qa_ref[g, :, col0:], (((0,), (0,)), ((), ())),
                            preferred_element_type=F32)
        s_refs[slot][g, :, col0:] = s
        bm_refs[slot][g, :, col0:] = jnp.max(s, axis=0, keepdims=True)

    def softmax_pv(j, slot, g, diag):
        col0 = 0 if diag is None else diag * tk
        off = pl.multiple_of(j * tk, tk)
        s = s_refs[slot][g, :, col0:]
        if diag is None:
            bm = bm_refs[slot][g]
        else:
            krow = lax.broadcasted_iota(jnp.int32, s.shape, 0)
            qcol = lax.broadcasted_iota(jnp.int32, s.shape, 1)
            s = jnp.where(krow <= qcol, s, NEG_INF)
            bm = jnp.max(s, axis=0, keepdims=True)
        m_prev = m_ref[g, :, col0:]
        m_new = jnp.maximum(m_prev, bm)
        alpha = jnp.exp2(m_prev - m_new)
        p = jnp.exp2(s - m_new).astype(BF16)
        va = jnp.concatenate([v_ref[0, g * d:(g + 1) * d, pl.ds(off, tk)], ones_rows], axis=0)
        acc_ref[g, :, col0:] = alpha * acc_ref[g, :, col0:] + jnp.dot(va, p, preferred_element_type=F32)
        m_ref[g, :, col0:] = m_new

    def step(j, slot, diag, next_col0=0):
        for g in range(g_heads):
            if next_col0 is not None:
                scores(j + 1, 1 - slot, g, next_col0)
            softmax_pv(j, slot, g, diag)

    for g in range(g_heads):
        scores(0, 0, g)

    def steps(j0, count):
        for u in range(count):
            step(j0 + u, u % 2, None)

    assert n_diag == 2
    rem = n_full % 4

    @pl.when(rem == 2)
    def _():
        steps(0, 2)

    def four_steps(jj, carry):
        steps(rem + 4 * jj, 4)
        return carry

    lax.fori_loop(0, n_full // 4, four_steps, 0)
    step(n_full, 0, 0, next_col0=tk)
    step(n_full + 1, 1, 1, next_col0=None)

    for g in range(g_heads):
        y = acc_ref[g, 0:d, :] * (1.0 / acc_ref[g, d:d + 1, :])
        o_ref[0, g * d:(g + 1) * d, :] = _gated(y, gate_ref[0, g * d:(g + 1) * d, :])


def _fox_attention(pt, qx, kx, *, n_heads, g_heads, tq, tk):
    bsz, _, s = pt.shape
    d = HEAD_DIM
    rows = g_heads * d
    nw = n_heads // g_heads
    blocks = 3 * rows * tq * 2 + 2 * rows * s * 2 + g_heads * XROWS * (tq + s) * 2
    scratch = g_heads * ((d + XROWS) * tq * 2 + 3 * 8 * tq * 4 + (d + XROWS) * tq * 4 + 2 * tk * tq * 4)
    return pl.pallas_call(
        functools.partial(_fox_attn_kernel, g_heads=g_heads, tq=tq, tk=tk),
        grid=(bsz, nw, s // tq),
        in_specs=[
            pl.BlockSpec((1, rows, tq), lambda b, hg, qi: (b, hg, qi)),
            pl.BlockSpec((1, rows, s), lambda b, hg, qi: (b, nw + hg, 0)),
            pl.BlockSpec((1, rows, s), lambda b, hg, qi: (b, 2 * nw + hg, 0)),
            pl.BlockSpec((1, rows, tq), lambda b, hg, qi: (b, 3 * nw + hg, qi)),
            pl.BlockSpec((1, g_heads, XROWS, tq), lambda b, hg, qi: (b, hg, 0, qi)),
            pl.BlockSpec((1, g_heads, XROWS, s), lambda b, hg, qi: (b, hg, 0, 0)),
        ],
        out_specs=pl.BlockSpec((1, rows, tq), lambda b, hg, qi: (b, hg, qi)),
        out_shape=jax.ShapeDtypeStruct((bsz, n_heads * d, s), BF16),
        scratch_shapes=[
            pltpu.VMEM((g_heads, d + XROWS, tq), BF16),
            pltpu.VMEM((g_heads, tk, tq), F32),
            pltpu.VMEM((g_heads, tk, tq), F32),
            pltpu.VMEM((g_heads, 1, tq), F32),
            pltpu.VMEM((g_heads, 1, tq), F32),
            pltpu.VMEM((g_heads, 1, tq), F32),
            pltpu.VMEM((g_heads, d + XROWS, tq), F32),
        ],
        compiler_params=pltpu.CompilerParams(
            dimension_semantics=("parallel", "parallel", "arbitrary"),
            vmem_limit_bytes=_vmem_limit(blocks, scratch, 4 * g_heads * tq * tk * 4),
        ),
        name="fox_attention",
    )(pt, pt, pt, pt, qx, kx)


def _swa_attn_kernel(q_ref, k_ref, v_ref, gate_ref, sink_ref, mask_ref, pick_ref, o_ref, *, blk, nsub):
    n = pl.program_id(2)
    d = HEAD_DIM
    g = q_ref.shape[1] // d
    ones_rows = (lax.broadcasted_iota(jnp.int32, (XROWS, 2 * blk), 0) == 0).astype(BF16)
    sink = sink_ref[0]

    def window_start(i):
        return pl.multiple_of(jnp.maximum(n * nsub + i - 1, 0) * blk, blk)

    scores = []
    for i in range(nsub):
        mask = jnp.where(n == 0, mask_ref[0], mask_ref[1]) if i == 0 else mask_ref[1]
        ka = jnp.concatenate([k_ref[0, :, pl.ds(window_start(i), 2 * blk)], mask], axis=0)
        qa = jnp.concatenate([q_ref[0, h * d:(h + 1) * d, i * blk:(i + 1) * blk] for h in range(g)], axis=1)
        qa = jnp.concatenate([qa, pick_ref[...]], axis=0)
        scores.append(lax.dot_general(ka, qa, (((0,), (0,)), ((), ())), preferred_element_type=F32))

    for i in range(nsub):
        start = window_start(i)
        s = scores[i]
        m = jnp.maximum(jnp.max(s, axis=0, keepdims=True), sink)
        e = jnp.exp2(s - m).astype(BF16)
        va = jnp.concatenate([v_ref[0, :, pl.ds(start, 2 * blk)], ones_rows], axis=0)
        o = jnp.dot(va, e, preferred_element_type=F32)
        denom = o[d:d + 1] + jnp.exp2(sink - m)
        y = o[0:d] * (1.0 / denom)
        for h in range(g):
            gate = gate_ref[0, h * d:(h + 1) * d, i * blk:(i + 1) * blk]
            o_ref[0, h * d:(h + 1) * d, i * blk:(i + 1) * blk] = _gated(y[:, h * blk:(h + 1) * blk], gate)


def _band_tables(blk):
    t = jnp.arange(blk)[:, None]
    j = jnp.arange(2 * blk)[None, :]

    def mask(first):
        diff = t - j + (0 if first else blk)
        return jnp.where((diff >= 0) & (diff < SWA_WINDOW), 0.0, NEG_INF)

    mask_t = jnp.stack([mask(True), mask(False)]).astype(BF16)
    pick = jnp.tile(jnp.eye(blk, dtype=BF16), (1, SWA_GROUP))
    return mask_t, pick


def _swa_attention(pt, sink_rows, *, blk, nsub):
    bsz, _, s = pt.shape
    d = HEAD_DIM
    rows = SWA_GROUP * d
    wq, wk = SWA_Q_HEADS * d, SWA_KV_HEADS * d
    tq = nsub * blk
    k0, v0, g0 = wq // d, (wq + wk) // d, (wq + 2 * wk) // rows
    mask_t, pick = _band_tables(blk)
    blocks = 3 * rows * tq * 2 + 2 * d * s * 2 + 8 * SWA_GROUP * blk * 4 + 4 * blk * blk * 2 + blk * rows * 2
    return pl.pallas_call(
        functools.partial(_swa_attn_kernel, blk=blk, nsub=nsub),
        grid=(bsz, SWA_KV_HEADS, s // tq),
        in_specs=[
            pl.BlockSpec((1, rows, tq), lambda b, h, n: (b, h, n)),
            pl.BlockSpec((1, d, s), lambda b, h, n: (b, k0 + h, 0)),
            pl.BlockSpec((1, d, s), lambda b, h, n: (b, v0 + h, 0)),
            pl.BlockSpec((1, rows, tq), lambda b, h, n: (b, g0 + h, n)),
            pl.BlockSpec((1, 1, SWA_GROUP * blk), lambda b, h, n: (h, 0, 0)),
            pl.BlockSpec((2, blk, 2 * blk), lambda b, h, n: (0, 0, 0)),
            pl.BlockSpec((blk, SWA_GROUP * blk), lambda b, h, n: (0, 0)),
        ],
        out_specs=pl.BlockSpec((1, rows, tq), lambda b, h, n: (b, h, n)),
        out_shape=jax.ShapeDtypeStruct((bsz, wq, s), BF16),
        compiler_params=pltpu.CompilerParams(
            dimension_semantics=("parallel", "parallel", "arbitrary"),
            vmem_limit_bytes=_vmem_limit(blocks, 0, 8 * nsub * 2 * blk * SWA_GROUP * blk * 4),
        ),
        name="swa_attention",
    )(pt, pt, pt, pt, sink_rows, mask_t, pick)


def _out_proj_kernel(zt_ref, x_ref, w_ref, fg_ref, o_ref, *, final_norm):
    out = x_ref[...] + lax.dot_general(zt_ref[0], w_ref[...], (((0,), (0,)), ((), ())),
                                       preferred_element_type=F32)
    if final_norm:
        ms = jnp.mean(out * out, axis=-1, keepdims=True)
        out = (out * lax.rsqrt(ms + RMS_EPS)) * fg_ref[...]
    o_ref[...] = out


def _out_proj(zt, x2d, w, fg, *, tm, final_norm):
    bsz, wid, s = zt.shape
    m, d = x2d.shape
    sb = s // tm
    blocks = wid * tm * 2 + 2 * tm * d * 4 + wid * d * 2 + d * 4
    return pl.pallas_call(
        functools.partial(_out_proj_kernel, final_norm=final_norm),
        grid=(m // tm,),
        in_specs=[
            pl.BlockSpec((1, wid, tm), lambda i: (i // sb, 0, i % sb)),
            pl.BlockSpec((tm, d), lambda i: (i, 0)),
            pl.BlockSpec((wid, d), lambda i: (0, 0)),
            pl.BlockSpec((1, d), lambda i: (0, 0)),
        ],
        out_specs=pl.BlockSpec((tm, d), lambda i: (i, 0)),
        out_shape=jax.ShapeDtypeStruct((m, d), F32),
        compiler_params=pltpu.CompilerParams(
            dimension_semantics=("parallel",),
            vmem_limit_bytes=_vmem_limit(blocks, 0, wid * tm * 2 + 3 * tm * d * 4),
        ),
        name="out_proj",
    )(zt, x2d, w, fg)


def _rope_tables(s):
    inv_freq = ROPE_THETA ** (-jnp.arange(ROT_HALF, dtype=F32) / ROT_HALF)
    ang = jnp.arange(s, dtype=F32)[:, None] * inv_freq[None, :]
    return jnp.cos(ang).T, jnp.sin(ang).T


PROJ_TM = 1024
FOX_PROJ_TN = 2048
SWA_PROJ_TN = 1536
WT_TN = 256
OUT_TM = 512
FOX_G, FOX_TQ, FOX_TK = 8, 512, 256
SWA_NSUB = 8


def kernel(x, norm_g, fox_w_in, fox_b_f, fox_w_out, swa_w_in, swa_sinks, swa_w_out, final_g):
    bsz, s, d = x.shape
    x2d = x.reshape(bsz * s, d)

    wid = FOX_HEADS * HEAD_DIM
    wt = fox_w_in[0].T.astype(BF16)
    pt, zt = _norm_proj(x2d, norm_g[0][None, :], wt, bsz=bsz, mode="fox", n_forget=FOX_HEADS,
                        tm=PROJ_TM, tn=FOX_PROJ_TN, q_blocks=wid // FOX_PROJ_TN)
    qx, kx = _decay_rows(zt, fox_b_f[0][:, None], FOX_HEADS)
    yt = _fox_attention(pt, qx, kx, n_heads=FOX_HEADS, g_heads=FOX_G, tq=FOX_TQ, tk=FOX_TK)
    x2d = _out_proj(yt, x2d, fox_w_out[0].astype(BF16), final_g[None, :], tm=OUT_TM, final_norm=False)

    wq = SWA_Q_HEADS * HEAD_DIM
    n_swa = swa_w_in.shape[2]
    wt = _transpose_cast(swa_w_in[0], n_swa, WT_TN)
    head = jnp.arange(n_swa // HEAD_DIM)
    head_scale = jnp.where(head < SWA_Q_HEADS, QK_SCALE, 1.0).astype(F32)
    head_rope = (head < SWA_Q_HEADS + SWA_KV_HEADS).astype(F32)
    (pt,) = _norm_proj(x2d, norm_g[1][None, :], wt, bsz=bsz, mode="swa", tm=PROJ_TM, tn=SWA_PROJ_TN,
                       tables=(head_scale, head_rope) + _rope_tables(s))
    sink_rows = jnp.repeat(swa_sinks[0].reshape(SWA_KV_HEADS, SWA_GROUP) * LOG2E, SWA_WINDOW, axis=1)[:, None, :]
    yt = _swa_attention(pt, sink_rows, blk=SWA_WINDOW, nsub=SWA_NSUB)
    out = _out_proj(yt, x2d, swa_w_out[0].astype(BF16), final_g[None, :], tm=OUT_TM, final_norm=True)
    return out.reshape(bsz, s, d)
```

```python
import functools
import math

import jax
import jax.numpy as jnp
from jax import lax
from jax.experimental import pallas as pl
from jax.experimental.pallas import tpu as pltpu

F32 = jnp.float32
BF16 = jnp.bfloat16

RMS_EPS = 1e-6
NEG_INF = -1e30
LOG2E = 1.4426950408889634
HEAD_DIM = 64
FOX_HEADS = 32
SWA_Q_HEADS = 32
SWA_KV_HEADS = 4
SWA_GROUP = SWA_Q_HEADS // SWA_KV_HEADS
SWA_WINDOW = 128
ROPE_THETA = 500000.0
ROT_DIM = HEAD_DIM // 4
ROT_HALF = ROT_DIM // 2
XROWS = 16
LANES = 128
VMEM_CAPACITY = 64 * 1024 * 1024
QK_SCALE = HEAD_DIM ** -0.5 * LOG2E


def _gated(y, gate):
    half = 0.5 * gate.astype(F32)
    return (y * (half * (1.0 + jnp.tanh(half)))).astype(gate.dtype)


def _vmem_limit(pipelined_block_bytes, scratch_bytes, temp_bytes):
    need = 2 * pipelined_block_bytes + scratch_bytes + temp_bytes
    return int(min(need, VMEM_CAPACITY - (4 << 20)))


_NT = (((1,), (1,)), ((), ()))


def _norm_to_scratch(x_ref, g_ref, h_ref):
    x = x_ref[...]
    ms = jnp.mean(x * x, axis=-1, keepdims=True)
    h_ref[...] = ((x * lax.rsqrt(ms + RMS_EPS)) * g_ref[...]).astype(BF16)


def _norm_proj_fox_kernel(x_ref, g_ref, wt_ref, wft_ref, o_ref, of_ref, h_ref, *, q_blocks):
    j = pl.program_id(1)

    @pl.when(j == 0)
    def _():
        _norm_to_scratch(x_ref, g_ref, h_ref)
        of_ref[0] = lax.dot_general(wft_ref[...], h_ref[...], _NT, preferred_element_type=F32)

    res = lax.dot_general(wt_ref[...], h_ref[...], _NT, preferred_element_type=F32)
    o_ref[0] = (res * jnp.where(j < q_blocks, QK_SCALE, 1.0)).astype(o_ref.dtype)


def _norm_proj_swa_kernel(scale_ref, rope_ref, x_ref, g_ref, wt_ref, cos_ref, sin_ref, o_ref, h_ref):
    j = pl.program_id(1)
    tn = o_ref.shape[1]
    heads = tn // HEAD_DIM

    @pl.when(j == 0)
    def _():
        _norm_to_scratch(x_ref, g_ref, h_ref)

    res = lax.dot_general(wt_ref[...], h_ref[...], _NT, preferred_element_type=F32)
    cos, sin = cos_ref[...], sin_ref[...]
    for hh in range(heads):
        base = hh * HEAD_DIM
        scale = scale_ref[j * heads + hh]
        rope = rope_ref[j * heads + hh]
        ch = (1.0 + rope * (cos - 1.0)) * scale
        sh = (rope * sin) * scale
        x1 = res[base:base + ROT_HALF]
        x2 = res[base + ROT_HALF:base + ROT_DIM]
        o_ref[0, base:base + ROT_DIM, :] = jnp.concatenate(
            [x1 * ch - x2 * sh, x2 * ch + x1 * sh], axis=0).astype(o_ref.dtype)
        o_ref[0, base + ROT_DIM:base + HEAD_DIM, :] = (res[base + ROT_DIM:base + HEAD_DIM] * scale).astype(o_ref.dtype)


def _transpose_cast_kernel(w_ref, o_ref):
    o_ref[...] = w_ref[...].T.astype(o_ref.dtype)


def _transpose_cast(w, n, tn):
    d = w.shape[0]
    return pl.pallas_call(
        _transpose_cast_kernel,
        grid=(n // tn,),
        in_specs=[pl.BlockSpec((d, tn), lambda j: (0, j))],
        out_specs=pl.BlockSpec((tn, d), lambda j: (j, 0)),
        out_shape=jax.ShapeDtypeStruct((n, d), BF16),
        compiler_params=pltpu.CompilerParams(
            dimension_semantics=("parallel",),
            vmem_limit_bytes=_vmem_limit(d * tn * 4 + tn * d * 2, 0, 2 * d * tn * 4),
        ),
        name="transpose_cast",
    )(w)


def _norm_proj(x2d, g, wt, *, bsz, mode, tm, tn, q_blocks=None, n_forget=0, tables=None):
    m, d = x2d.shape
    n = wt.shape[0] - n_forget
    s = m // bsz
    sb = s // tm
    in_specs = [
        pl.BlockSpec((tm, d), lambda i, j: (i, 0)),
        pl.BlockSpec((1, d), lambda i, j: (0, 0)),
        pl.BlockSpec((tn, d), lambda i, j: (j, 0)),
    ]
    args = [x2d, g, wt]
    out_specs = [pl.BlockSpec((1, tn, tm), lambda i, j: (i // sb, j, i % sb))]
    out_shape = [jax.ShapeDtypeStruct((bsz, n, s), BF16)]
    blocks = tm * d * 4 + d * 4 + tn * d * 2 + tn * tm * 2
    if mode == "fox":
        body = functools.partial(_norm_proj_fox_kernel, q_blocks=q_blocks)
        in_specs.append(pl.BlockSpec((n_forget, d), lambda i, j: (n // n_forget, 0)))
        args.append(wt)
        out_specs.append(pl.BlockSpec((1, n_forget, tm), lambda i, j: (i // sb, 0, i % sb)))
        out_shape.append(jax.ShapeDtypeStruct((bsz, n_forget, s), F32))
        blocks += n_forget * d * 2 + n_forget * tm * 4
    else:
        body = _norm_proj_swa_kernel
        scale, rope, cos, sin = tables
        in_specs = [pl.BlockSpec(memory_space=pltpu.SMEM)] * 2 + in_specs
        in_specs += [pl.BlockSpec((ROT_HALF, tm), lambda i, j: (0, i % sb))] * 2
        args = [scale, rope] + args + [cos, sin]
        blocks += 2 * ROT_HALF * tm * 4
    return pl.pallas_call(
        body,
        grid=(m // tm, n // tn),
        in_specs=in_specs,
        out_specs=out_specs,
        out_shape=out_shape,
        scratch_shapes=[pltpu.VMEM((tm, d), BF16)],
        compiler_params=pltpu.CompilerParams(
            dimension_semantics=("parallel", "arbitrary"),
            vmem_limit_bytes=_vmem_limit(blocks, tm * d * 2, 3 * tm * d * 4 + 2 * tn * tm * 4),
        ),
        name="norm_proj_" + mode,
    )(*args)


def _decay_rows_kernel(z_ref, b_ref, qx_ref, kx_ref):
    z = z_ref[0] + b_ref[...]
    x = (jnp.minimum(z, 0.0) - jnp.log1p(jnp.exp(-jnp.abs(z)))) * LOG2E
    n_heads, s = x.shape
    col = lax.broadcasted_iota(jnp.int32, x.shape, 1)
    shift = 1
    while shift < s:
        x = x + jnp.where(col >= shift, pltpu.roll(x, shift, axis=1), 0.0)
        shift *= 2
    hi = x.astype(BF16).astype(F32)
    mid = (x - hi).astype(BF16).astype(F32)
    lo = (x - hi - mid).astype(BF16).astype(F32)
    row = lax.broadcasted_iota(jnp.int32, (XROWS, s), 0)
    for h in range(n_heads):
        parts_q = jnp.where(row == 0, hi[h:h + 1], jnp.where(row == 1, mid[h:h + 1], lo[h:h + 1]))
        parts_k = jnp.where(row == 3, hi[h:h + 1], jnp.where(row == 4, mid[h:h + 1], lo[h:h + 1]))
        qx = jnp.where(row < 3, parts_q, jnp.where(row < 6, 1.0, 0.0))
        kx = jnp.where(row < 3, 1.0, jnp.where(row < 6, -parts_k, 0.0))
        qx_ref[0, h] = qx.astype(BF16)
        kx_ref[0, h] = kx.astype(BF16)


def _decay_rows(z, b, n_heads):
    bsz, _, s = z.shape
    out_blk = n_heads * XROWS * s * 2
    return pl.pallas_call(
        _decay_rows_kernel,
        grid=(bsz,),
        in_specs=[
            pl.BlockSpec((1, n_heads, s), lambda i: (i, 0, 0)),
            pl.BlockSpec((n_heads, 1), lambda i: (0, 0)),
        ],
        out_specs=[pl.BlockSpec((1, n_heads, XROWS, s), lambda i: (i, 0, 0, 0))] * 2,
        out_shape=[jax.ShapeDtypeStruct((bsz, n_heads, XROWS, s), BF16)] * 2,
        compiler_params=pltpu.CompilerParams(
            dimension_semantics=("parallel",),
            vmem_limit_bytes=_vmem_limit(n_heads * s * 4 + 2 * out_blk, 0, 8 * n_heads * s * 4),
        ),
        name="decay_rows",
    )(z, b)


def _fox_attn_kernel(q_ref, k_ref, v_ref, gate_ref, qx_ref, kx_ref, o_ref,
                     qa_ref, s0_ref, s1_ref, bm0_ref, bm1_ref, m_ref, acc_ref, *, g_heads, tq, tk, n_sub):
    for sub in range(n_sub):
        cols = slice(sub * tq, (sub + 1) * tq)
        _fox_query_block(pl.program_id(2) * n_sub + sub,
                         q_ref.at[:, :, cols], k_ref, v_ref, gate_ref.at[:, :, cols],
                         qx_ref.at[:, :, :, cols], kx_ref, o_ref.at[:, :, cols],
                         qa_ref.at[sub], s0_ref, s1_ref, bm0_ref, bm1_ref, m_ref.at[sub], acc_ref.at[sub],
                         g_heads=g_heads, tq=tq, tk=tk)


def _fox_query_block(qi, q_ref, k_ref, v_ref, gate_ref, qx_ref, kx_ref, o_ref,
                     qa_ref, s0_ref, s1_ref, bm0_ref, bm1_ref, m_ref, acc_ref, *, g_heads, tq, tk):
    n_diag = tq // tk
    n_full = qi * n_diag
    d = HEAD_DIM
    s_refs = (s0_ref, s1_ref)
    bm_refs = (bm0_ref, bm1_ref)

    for g in range(g_heads):
        qa_ref[g, 0:d, :] = q_ref[0, g * d:(g + 1) * d, :]
        qa_ref[g, d:d + XROWS, :] = qx_ref[0, g]
    m_ref[...] = jnp.full(m_ref.shape, NEG_INF, F32)
    acc_ref[...] = jnp.zeros(acc_ref.shape, F32)

    ones_rows = (lax.broadcasted_iota(jnp.int32, (XROWS, tk), 0) == 0).astype(BF16)

    def scores(j, slot, g, col0=0):
        off = pl.multiple_of(j * tk, tk)
        ka = jnp.concatenate([k_ref[0, g * d:(g + 1) * d, pl.ds(off, tk)],
                              kx_ref[0, g, :, pl.ds(off, tk)]], axis=0)
        s = lax.dot_general(ka, qa_ref[g, :, col0:], (((0,), (0,)), ((), ())),
                            preferred_element_type=F32)
        s_refs[slot][g, :, col0:] = s
        bm_refs[slot][g, :, col0:] = jnp.max(s, axis=0, keepdims=True)

    def softmax_pv(j, slot, g, diag):
        col0 = 0 if diag is None else diag * tk
        off = pl.multiple_of(j * tk, tk)
        s = s_refs[slot][g, :, col0:]
        if diag is None:
            bm = bm_refs[slot][g]
        else:
            krow = lax.broadcasted_iota(jnp.int32, s.shape, 0)
            qcol = lax.broadcasted_iota(jnp.int32, s.shape, 1)
            s = jnp.where(krow <= qcol, s, NEG_INF)
            bm = jnp.max(s, axis=0, keepdims=True)
        m_prev = m_ref[g, :, col0:]
        m_new = jnp.maximum(m_prev, bm)
        alpha = jnp.exp2(m_prev - m_new)
        p = jnp.exp2(s - m_new).astype(BF16)
        va = jnp.concatenate([v_ref[0, g * d:(g + 1) * d, pl.ds(off, tk)], ones_rows], axis=0)
        acc_ref[g, :, col0:] = alpha * acc_ref[g, :, col0:] + jnp.dot(va, p, preferred_element_type=F32)
        m_ref[g, :, col0:] = m_new

    def step(j, slot, diag, next_col0=0):
        for g in range(g_heads):
            if next_col0 is not None:
                scores(j + 1, 1 - slot, g, next_col0)
            softmax_pv(j, slot, g, diag)

    for g in range(g_heads):
        scores(0, 0, g)

    def steps(j0, count):
        for u in range(count):
            step(j0 + u, u % 2, None)

    assert n_diag == 2
    rem = n_full % 4

    @pl.when(rem == 2)
    def _():
        steps(0, 2)

    def four_steps(jj, carry):
        steps(rem + 4 * jj, 4)
        return carry

    lax.fori_loop(0, n_full // 4, four_steps, 0)
    step(n_full, 0, 0, next_col0=tk)
    step(n_full + 1, 1, 1, next_col0=None)

    for g in range(g_heads):
        y = acc_ref[g, 0:d, :] * (1.0 / acc_ref[g, d:d + 1, :])
        o_ref[0, g * d:(g + 1) * d, :] = _gated(y, gate_ref[0, g * d:(g + 1) * d, :])


def _fox_attention(pt, qx, kx, *, n_heads, g_heads, tq, tk, n_sub):
    bsz, _, s = pt.shape
    d = HEAD_DIM
    rows = g_heads * d
    nw = n_heads // g_heads
    tqs = n_sub * tq
    blocks = 3 * rows * tqs * 2 + 2 * rows * s * 2 + g_heads * XROWS * (tqs + s) * 2
    scratch = g_heads * (n_sub * ((d + XROWS) * tq * 2 + 8 * tq * 4 + (d + XROWS) * tq * 4)
                         + 2 * 8 * tq * 4 + 2 * tk * tq * 4)
    return pl.pallas_call(
        functools.partial(_fox_attn_kernel, g_heads=g_heads, tq=tq, tk=tk, n_sub=n_sub),
        grid=(bsz, nw, s // tqs),
        in_specs=[
            pl.BlockSpec((1, rows, tqs), lambda b, hg, qi: (b, hg, qi)),
            pl.BlockSpec((1, rows, s), lambda b, hg, qi: (b, nw + hg, 0)),
            pl.BlockSpec((1, rows, s), lambda b, hg, qi: (b, 2 * nw + hg, 0)),
            pl.BlockSpec((1, rows, tqs), lambda b, hg, qi: (b, 3 * nw + hg, qi)),
            pl.BlockSpec((1, g_heads, XROWS, tqs), lambda b, hg, qi: (b, hg, 0, qi)),
            pl.BlockSpec((1, g_heads, XROWS, s), lambda b, hg, qi: (b, hg, 0, 0)),
        ],
        out_specs=pl.BlockSpec((1, rows, tqs), lambda b, hg, qi: (b, hg, qi)),
        out_shape=jax.ShapeDtypeStruct((bsz, n_heads * d, s), BF16),
        scratch_shapes=[
            pltpu.VMEM((n_sub, g_heads, d + XROWS, tq), BF16),
            pltpu.VMEM((g_heads, tk, tq), F32),
            pltpu.VMEM((g_heads, tk, tq), F32),
            pltpu.VMEM((g_heads, 1, tq), F32),
            pltpu.VMEM((g_heads, 1, tq), F32),
            pltpu.VMEM((n_sub, g_heads, 1, tq), F32),
            pltpu.VMEM((n_sub, g_heads, d + XROWS, tq), F32),
        ],
        compiler_params=pltpu.CompilerParams(
            dimension_semantics=("parallel", "parallel", "arbitrary"),
            vmem_limit_bytes=_vmem_limit(blocks, scratch, 4 * g_heads * tq * tk * 4),
        ),
        name="fox_attention",
    )(pt, pt, pt, pt, qx, kx)


def _swa_attn_kernel(q_ref, k_ref, v_ref, gate_ref, sink_ref, mask_ref, pick_ref, o_ref, *, blk, nsub):
    n = pl.program_id(2)
    d = HEAD_DIM
    g = q_ref.shape[1] // d
    ones_rows = (lax.broadcasted_iota(jnp.int32, (XROWS, 2 * blk), 0) == 0).astype(BF16)
    sink = sink_ref[0]

    def window_start(i):
        return pl.multiple_of(jnp.maximum(n * nsub + i - 1, 0) * blk, blk)

    scores = []
    for i in range(nsub):
        mask = jnp.where(n == 0, mask_ref[0], mask_ref[1]) if i == 0 else mask_ref[1]
        ka = jnp.concatenate([k_ref[0, :, pl.ds(window_start(i), 2 * blk)], mask], axis=0)
        qa = jnp.concatenate([q_ref[0, h * d:(h + 1) * d, i * blk:(i + 1) * blk] for h in range(g)], axis=1)
        qa = jnp.concatenate([qa, pick_ref[...]], axis=0)
        scores.append(lax.dot_general(ka, qa, (((0,), (0,)), ((), ())), preferred_element_type=F32))

    for i in range(nsub):
        start = window_start(i)
        s = scores[i]
        m = jnp.maximum(jnp.max(s, axis=0, keepdims=True), sink)
        e = jnp.exp2(s - m).astype(BF16)
        va = jnp.concatenate([v_ref[0, :, pl.ds(start, 2 * blk)], ones_rows], axis=0)
        o = jnp.dot(va, e, preferred_element_type=F32)
        denom = o[d:d + 1] + jnp.exp2(sink - m)
        y = o[0:d] * (1.0 / denom)
        for h in range(g):
            gate = gate_ref[0, h * d:(h + 1) * d, i * blk:(i + 1) * blk]
            o_ref[0, h * d:(h + 1) * d, i * blk:(i + 1) * blk] = _gated(y[:, h * blk:(h + 1) * blk], gate)


def _band_tables(blk):
    t = jnp.arange(blk)[:, None]
    j = jnp.arange(2 * blk)[None, :]

    def mask(first):
        diff = t - j + (0 if first else blk)
        return jnp.where((diff >= 0) & (diff < SWA_WINDOW), 0.0, NEG_INF)

    mask_t = jnp.stack([mask(True), mask(False)]).astype(BF16)
    pick = jnp.tile(jnp.eye(blk, dtype=BF16), (1, SWA_GROUP))
    return mask_t, pick


def _swa_attention(pt, sink_rows, *, blk, nsub):
    bsz, _, s = pt.shape
    d = HEAD_DIM
    rows = SWA_GROUP * d
    wq, wk = SWA_Q_HEADS * d, SWA_KV_HEADS * d
    tq = nsub * blk
    k0, v0, g0 = wq // d, (wq + wk) // d, (wq + 2 * wk) // rows
    mask_t, pick = _band_tables(blk)
    blocks = 3 * rows * tq * 2 + 2 * d * s * 2 + 8 * SWA_GROUP * blk * 4 + 4 * blk * blk * 2 + blk * rows * 2
    return pl.pallas_call(
        functools.partial(_swa_attn_kernel, blk=blk, nsub=nsub),
        grid=(bsz, SWA_KV_HEADS, s // tq),
        in_specs=[
            pl.BlockSpec((1, rows, tq), lambda b, h, n: (b, h, n)),
            pl.BlockSpec((1, d, s), lambda b, h, n: (b, k0 + h, 0)),
            pl.BlockSpec((1, d, s), lambda b, h, n: (b, v0 + h, 0)),
            pl.BlockSpec((1, rows, tq), lambda b, h, n: (b, g0 + h, n)),
            pl.BlockSpec((1, 1, SWA_GROUP * blk), lambda b, h, n: (h, 0, 0)),
            pl.BlockSpec((2, blk, 2 * blk), lambda b, h, n: (0, 0, 0)),
            pl.BlockSpec((blk, SWA_GROUP * blk), lambda b, h, n: (0, 0)),
        ],
        out_specs=pl.BlockSpec((1, rows, tq), lambda b, h, n: (b, h, n)),
        out_shape=jax.ShapeDtypeStruct((bsz, wq, s), BF16),
        compiler_params=pltpu.CompilerParams(
            dimension_semantics=("parallel", "parallel", "arbitrary"),
            vmem_limit_bytes=_vmem_limit(blocks, 0, 8 * nsub * 2 * blk * SWA_GROUP * blk * 4),
        ),
        name="swa_attention",
    )(pt, pt, pt, pt, sink_rows, mask_t, pick)


def _out_proj_kernel(zt_ref, x_ref, w_ref, fg_ref, o_ref, *, final_norm):
    out = x_ref[...] + lax.dot_general(zt_ref[0], w_ref[...], (((0,), (0,)), ((), ())),
                                       preferred_element_type=F32)
    if final_norm:
        ms = jnp.mean(out * out, axis=-1, keepdims=True)
        out = (out * lax.rsqrt(ms + RMS_EPS)) * fg_ref[...]
    o_ref[...] = out


def _out_proj(zt, x2d, w, fg, *, tm, final_norm):
    bsz, wid, s = zt.shape
    m, d = x2d.shape
    sb = s // tm
    blocks = wid * tm * 2 + 2 * tm * d * 4 + wid * d * 2 + d * 4
    return pl.pallas_call(
        functools.partial(_out_proj_kernel, final_norm=final_norm),
        grid=(m // tm,),
        in_specs=[
            pl.BlockSpec((1, wid, tm), lambda i: (i // sb, 0, i % sb)),
            pl.BlockSpec((tm, d), lambda i: (i, 0)),
            pl.BlockSpec((wid, d), lambda i: (0, 0)),
            pl.BlockSpec((1, d), lambda i: (0, 0)),
        ],
        out_specs=pl.BlockSpec((tm, d), lambda i: (i, 0)),
        out_shape=jax.ShapeDtypeStruct((m, d), F32),
        compiler_params=pltpu.CompilerParams(
            dimension_semantics=("parallel",),
            vmem_limit_bytes=_vmem_limit(blocks, 0, wid * tm * 2 + 3 * tm * d * 4),
        ),
        name="out_proj",
    )(zt, x2d, w, fg)


def _rope_tables(s):
    inv_freq = ROPE_THETA ** (-jnp.arange(ROT_HALF, dtype=F32) / ROT_HALF)
    ang = jnp.arange(s, dtype=F32)[:, None] * inv_freq[None, :]
    return jnp.cos(ang).T, jnp.sin(ang).T


PROJ_TM = 1024
FOX_PROJ_TN = 2048
SWA_PROJ_TN = 1536
WT_TN = 256
OUT_TM = 512
FOX_G, FOX_TQ, FOX_TK = 8, 512, 256
FOX_NSUB = 2
SWA_NSUB = 8


def kernel(x, norm_g, fox_w_in, fox_b_f, fox_w_out, swa_w_in, swa_sinks, swa_w_out, final_g):
    bsz, s, d = x.shape
    x2d = x.reshape(bsz * s, d)

    wid = FOX_HEADS * HEAD_DIM
    wt = fox_w_in[0].T.astype(BF16)
    pt, zt = _norm_proj(x2d, norm_g[0][None, :], wt, bsz=bsz, mode="fox", n_forget=FOX_HEADS,
                        tm=PROJ_TM, tn=FOX_PROJ_TN, q_blocks=wid // FOX_PROJ_TN)
    qx, kx = _decay_rows(zt, fox_b_f[0][:, None], FOX_HEADS)
    yt = _fox_attention(pt, qx, kx, n_heads=FOX_HEADS, g_heads=FOX_G, tq=FOX_TQ, tk=FOX_TK, n_sub=FOX_NSUB)
    x2d = _out_proj(yt, x2d, fox_w_out[0].astype(BF16), final_g[None, :], tm=OUT_TM, final_norm=False)

    wq = SWA_Q_HEADS * HEAD_DIM
    n_swa = swa_w_in.shape[2]
    wt = _transpose_cast(swa_w_in[0], n_swa, WT_TN)
    head = jnp.arange(n_swa // HEAD_DIM)
    head_scale = jnp.where(head < SWA_Q_HEADS, QK_SCALE, 1.0).astype(F32)
    head_rope = (head < SWA_Q_HEADS + SWA_KV_HEADS).astype(F32)
    (pt,) = _norm_proj(x2d, norm_g[1][None, :], wt, bsz=bsz, mode="swa", tm=PROJ_TM, tn=SWA_PROJ_TN,
                       tables=(head_scale, head_rope) + _rope_tables(s))
    sink_rows = jnp.repeat(swa_sinks[0].reshape(SWA_KV_HEADS, SWA_GROUP) * LOG2E, SWA_WINDOW, axis=1)[:, None, :]
    yt = _swa_attention(pt, sink_rows, blk=SWA_WINDOW, nsub=SWA_NSUB)
    out = _out_proj(yt, x2d, swa_w_out[0].astype(BF16), final_g[None, :], tm=OUT_TM, final_norm=True)
    return out.reshape(bsz, s, d)
```

```python
import functools

import jax
import jax.numpy as jnp
from jax import lax
from jax.experimental import pallas as pl
from jax.experimental.pallas import tpu as pltpu

F32 = jnp.float32
BF16 = jnp.bfloat16

RMS_EPS = 1e-6
NEG_INF = -1e30
LOG2E = 1.4426950408889634
HEAD_DIM = 64
FOX_HEADS = 32
SWA_Q_HEADS = 32
SWA_KV_HEADS = 4
SWA_GROUP = SWA_Q_HEADS // SWA_KV_HEADS
SWA_WINDOW = 128
ROPE_THETA = 500000.0
ROT_DIM = HEAD_DIM // 4
ROT_HALF = ROT_DIM // 2
XROWS = 16
LANES = 128
VMEM_CAPACITY = 64 * 1024 * 1024
QK_SCALE = HEAD_DIM ** -0.5 * LOG2E


def _gated(y, gate):
    half = 0.5 * gate.astype(F32)
    return (y * (half * (1.0 + jnp.tanh(half)))).astype(gate.dtype)


def _vmem_limit(pipelined_block_bytes, scratch_bytes, temp_bytes):
    need = 2 * pipelined_block_bytes + scratch_bytes + temp_bytes
    return int(min(need, VMEM_CAPACITY - (4 << 20)))


_NT = (((1,), (1,)), ((), ()))


def _norm_to_scratch(x_ref, g_ref, h_ref):
    x = x_ref[...]
    ms = jnp.mean(x * x, axis=-1, keepdims=True)
    h_ref[...] = ((x * lax.rsqrt(ms + RMS_EPS)) * g_ref[...]).astype(BF16)


def _norm_proj_fox_kernel(x_ref, g_ref, wt_ref, wft_ref, o_ref, of_ref, h_ref, *, q_blocks):
    j = pl.program_id(1)

    @pl.when(j == 0)
    def _():
        _norm_to_scratch(x_ref, g_ref, h_ref)
        of_ref[0] = lax.dot_general(wft_ref[...], h_ref[...], _NT, preferred_element_type=F32)

    res = lax.dot_general(wt_ref[...], h_ref[...], _NT, preferred_element_type=F32)
    o_ref[0] = (res * jnp.where(j < q_blocks, QK_SCALE, 1.0)).astype(o_ref.dtype)


def _norm_proj_swa_kernel(scale_ref, rope_ref, x_ref, g_ref, wt_ref, cos_ref, sin_ref, o_ref, h_ref):
    j = pl.program_id(1)
    tn = o_ref.shape[1]
    heads = tn // HEAD_DIM

    @pl.when(j == 0)
    def _():
        _norm_to_scratch(x_ref, g_ref, h_ref)

    res = lax.dot_general(wt_ref[...], h_ref[...], _NT, preferred_element_type=F32)
    cos, sin = cos_ref[...], sin_ref[...]
    for hh in range(heads):
        base = hh * HEAD_DIM
        scale = scale_ref[j * heads + hh]
        rope = rope_ref[j * heads + hh]
        ch = (1.0 + rope * (cos - 1.0)) * scale
        sh = (rope * sin) * scale
        x1 = res[base:base + ROT_HALF]
        x2 = res[base + ROT_HALF:base + ROT_DIM]
        o_ref[0, base:base + ROT_DIM, :] = jnp.concatenate(
            [x1 * ch - x2 * sh, x2 * ch + x1 * sh], axis=0).astype(o_ref.dtype)
        o_ref[0, base + ROT_DIM:base + HEAD_DIM, :] = (res[base + ROT_DIM:base + HEAD_DIM] * scale).astype(o_ref.dtype)


def _transpose_cast_kernel(w_ref, o_ref):
    o_ref[...] = w_ref[...].T.astype(o_ref.dtype)


def _transpose_cast(w, n, tn):
    d = w.shape[0]
    return pl.pallas_call(
        _transpose_cast_kernel,
        grid=(n // tn,),
        in_specs=[pl.BlockSpec((d, tn), lambda j: (0, j))],
        out_specs=pl.BlockSpec((tn, d), lambda j: (j, 0)),
        out_shape=jax.ShapeDtypeStruct((n, d), BF16),
        compiler_params=pltpu.CompilerParams(
            dimension_semantics=("parallel",),
            vmem_limit_bytes=_vmem_limit(d * tn * 4 + tn * d * 2, 0, 2 * d * tn * 4),
        ),
        name="transpose_cast",
    )(w)


def _norm_proj(x2d, g, wt, *, bsz, mode, tm, tn, q_blocks=None, n_forget=0, tables=None):
    m, d = x2d.shape
    n = wt.shape[0] - n_forget
    s = m // bsz
    sb = s // tm
    in_specs = [
        pl.BlockSpec((tm, d), lambda i, j: (i, 0)),
        pl.BlockSpec((1, d), lambda i, j: (0, 0)),
        pl.BlockSpec((tn, d), lambda i, j: (j, 0)),
    ]
    args = [x2d, g, wt]
    out_specs = [pl.BlockSpec((1, tn, tm), lambda i, j: (i // sb, j, i % sb))]
    out_shape = [jax.ShapeDtypeStruct((bsz, n, s), BF16)]
    blocks = tm * d * 4 + d * 4 + tn * d * 2 + tn * tm * 2
    if mode == "fox":
        body = functools.partial(_norm_proj_fox_kernel, q_blocks=q_blocks)
        in_specs.append(pl.BlockSpec((n_forget, d), lambda i, j: (n // n_forget, 0)))
        args.append(wt)
        out_specs.append(pl.BlockSpec((1, n_forget, tm), lambda i, j: (i // sb, 0, i % sb)))
        out_shape.append(jax.ShapeDtypeStruct((bsz, n_forget, s), F32))
        blocks += n_forget * d * 2 + n_forget * tm * 4
    else:
        body = _norm_proj_swa_kernel
        scale, rope, cos, sin = tables
        in_specs = [pl.BlockSpec(memory_space=pltpu.SMEM)] * 2 + in_specs
        in_specs += [pl.BlockSpec((ROT_HALF, tm), lambda i, j: (0, i % sb))] * 2
        args = [scale, rope] + args + [cos, sin]
        blocks += 2 * ROT_HALF * tm * 4
    return pl.pallas_call(
        body,
        grid=(m // tm, n // tn),
        in_specs=in_specs,
        out_specs=out_specs,
        out_shape=out_shape,
        scratch_shapes=[pltpu.VMEM((tm, d), BF16)],
        compiler_params=pltpu.CompilerParams(
            dimension_semantics=("parallel", "arbitrary"),
            vmem_limit_bytes=_vmem_limit(blocks, tm * d * 2, 3 * tm * d * 4 + 2 * tn * tm * 4),
        ),
        name="norm_proj_" + mode,
    )(*args)


def _decay_rows_kernel(z_ref, b_ref, qx_ref, kx_ref):
    z = z_ref[0] + b_ref[...]
    x = (jnp.minimum(z, 0.0) - jnp.log1p(jnp.exp(-jnp.abs(z)))) * LOG2E
    n_heads, s = x.shape
    col = lax.broadcasted_iota(jnp.int32, x.shape, 1)
    shift = 1
    while shift < s:
        x = x + jnp.where(col >= shift, pltpu.roll(x, shift, axis=1), 0.0)
        shift *= 2
    hi = x.astype(BF16).astype(F32)
    mid = (x - hi).astype(BF16).astype(F32)
    lo = (x - hi - mid).astype(BF16).astype(F32)
    row = lax.broadcasted_iota(jnp.int32, (XROWS, s), 0)
    for h in range(n_heads):
        parts_q = jnp.where(row == 0, hi[h:h + 1], jnp.where(row == 1, mid[h:h + 1], lo[h:h + 1]))
        parts_k = jnp.where(row == 3, hi[h:h + 1], jnp.where(row == 4, mid[h:h + 1], lo[h:h + 1]))
        qx = jnp.where(row < 3, parts_q, jnp.where(row < 6, 1.0, 0.0))
        kx = jnp.where(row < 3, 1.0, jnp.where(row < 6, -parts_k, 0.0))
        qx_ref[0, h] = qx.astype(BF16)
        kx_ref[0, h] = kx.astype(BF16)


def _decay_rows(z, b, n_heads):
    bsz, _, s = z.shape
    out_blk = n_heads * XROWS * s * 2
    return pl.pallas_call(
        _decay_rows_kernel,
        grid=(bsz,),
        in_specs=[
            pl.BlockSpec((1, n_heads, s), lambda i: (i, 0, 0)),
            pl.BlockSpec((n_heads, 1), lambda i: (0, 0)),
        ],
        out_specs=[pl.BlockSpec((1, n_heads, XROWS, s), lambda i: (i, 0, 0, 0))] * 2,
        out_shape=[jax.ShapeDtypeStruct((bsz, n_heads, XROWS, s), BF16)] * 2,
        compiler_params=pltpu.CompilerParams(
            dimension_semantics=("parallel",),
            vmem_limit_bytes=_vmem_limit(n_heads * s * 4 + 2 * out_blk, 0, 8 * n_heads * s * 4),
        ),
        name="decay_rows",
    )(z, b)


def _fox_attn_kernel(q_ref, k_ref, v_ref, gate_ref, qx_ref, kx_ref, o_ref,
                     qa_ref, s0_ref, s1_ref, bm0_ref, bm1_ref, m_ref, acc_ref, *, g_heads, tq, tk, n_sub):
    for sub in range(n_sub):
        cols = slice(sub * tq, (sub + 1) * tq)
        _fox_query_block(pl.program_id(2) * n_sub + sub,
                         q_ref.at[:, :, cols], k_ref, v_ref, gate_ref.at[:, :, cols],
                         qx_ref.at[:, :, :, cols], kx_ref, o_ref.at[:, :, cols],
                         qa_ref.at[sub], s0_ref, s1_ref, bm0_ref, bm1_ref, m_ref.at[sub], acc_ref.at[sub],
                         g_heads=g_heads, tq=tq, tk=tk)


def _fox_query_block(qi, q_ref, k_ref, v_ref, gate_ref, qx_ref, kx_ref, o_ref,
                     qa_ref, s0_ref, s1_ref, bm0_ref, bm1_ref, m_ref, acc_ref, *, g_heads, tq, tk):
    n_diag = tq // tk
    n_full = qi * n_diag
    d = HEAD_DIM
    s_refs = (s0_ref, s1_ref)
    bm_refs = (bm0_ref, bm1_ref)

    for g in range(g_heads):
        qa_ref[g, 0:d, :] = q_ref[0, g * d:(g + 1) * d, :]
        qa_ref[g, d:d + XROWS, :] = qx_ref[0, g]
    m_ref[...] = jnp.full(m_ref.shape, NEG_INF, F32)
    acc_ref[...] = jnp.zeros(acc_ref.shape, F32)

    ones_rows = (lax.broadcasted_iota(jnp.int32, (XROWS, tk), 0) == 0).astype(BF16)

    def scores(j, slot, g, col0=0):
        off = pl.multiple_of(j * tk, tk)
        ka = jnp.concatenate([k_ref[0, g * d:(g + 1) * d, pl.ds(off, tk)],
                              kx_ref[0, g, :, pl.ds(off, tk)]], axis=0)
        s = lax.dot_general(ka, qa_ref[g, :, col0:], (((0,), (0,)), ((), ())),
                            preferred_element_type=F32)
        s_refs[slot][g, :, col0:] = s
        bm_refs[slot][g, :, col0:] = jnp.max(s, axis=0, keepdims=True)

    def softmax_pv(j, slot, g, diag):
        col0 = 0 if diag is None else diag * tk
        off = pl.multiple_of(j * tk, tk)
        s = s_refs[slot][g, :, col0:]
        if diag is None:
            bm = bm_refs[slot][g]
        else:
            krow = lax.broadcasted_iota(jnp.int32, s.shape, 0)
            qcol = lax.broadcasted_iota(jnp.int32, s.shape, 1)
            s = jnp.where(krow <= qcol, s, NEG_INF)
            bm = jnp.max(s, axis=0, keepdims=True)
        m_prev = m_ref[g, :, col0:]
        m_new = jnp.maximum(m_prev, bm)
        alpha = jnp.exp2(m_prev - m_new)
        p = jnp.exp2(s - m_new).astype(BF16)
        va = jnp.concatenate([v_ref[0, g * d:(g + 1) * d, pl.ds(off, tk)], ones_rows], axis=0)
        acc_ref[g, :, col0:] = alpha * acc_ref[g, :, col0:] + jnp.dot(va, p, preferred_element_type=F32)
        m_ref[g, :, col0:] = m_new

    def step(j, slot, diag, next_col0=0):
        for g in range(g_heads):
            if next_col0 is not None:
                scores(j + 1, 1 - slot, g, next_col0)
            softmax_pv(j, slot, g, diag)

    for g in range(g_heads):
        scores(0, 0, g)

    def steps(j0, count):
        for u in range(count):
            step(j0 + u, u % 2, None)

    assert n_diag == 2
    rem = n_full % 4

    @pl.when(rem == 2)
    def _():
        steps(0, 2)

    def four_steps(jj, carry):
        steps(rem + 4 * jj, 4)
        return carry

    lax.fori_loop(0, n_full // 4, four_steps, 0)
    step(n_full, 0, 0, next_col0=tk)
    step(n_full + 1, 1, 1, next_col0=None)

    for g in range(g_heads):
        y = acc_ref[g, 0:d, :] * (1.0 / acc_ref[g, d:d + 1, :])
        o_ref[0, g * d:(g + 1) * d, :] = _gated(y, gate_ref[0, g * d:(g + 1) * d, :])


def _fox_attention(pt, qx, kx, *, n_heads, g_heads, tq, tk, n_sub):
    bsz, _, s = pt.shape
    d = HEAD_DIM
    rows = g_heads * d
    nw = n_heads // g_heads
    tqs = n_sub * tq
    blocks = 3 * rows * tqs * 2 + 2 * rows * s * 2 + g_heads * XROWS * (tqs + s) * 2
    scratch = g_heads * (n_sub * ((d + XROWS) * tq * 2 + 8 * tq * 4 + (d + XROWS) * tq * 4)
                         + 2 * 8 * tq * 4 + 2 * tk * tq * 4)
    return pl.pallas_call(
        functools.partial(_fox_attn_kernel, g_heads=g_heads, tq=tq, tk=tk, n_sub=n_sub),
        grid=(bsz, nw, s // tqs),
        in_specs=[
            pl.BlockSpec((1, rows, tqs), lambda b, hg, qi: (b, hg, qi)),
            pl.BlockSpec((1, rows, s), lambda b, hg, qi: (b, nw + hg, 0)),
            pl.BlockSpec((1, rows, s), lambda b, hg, qi: (b, 2 * nw + hg, 0)),
            pl.BlockSpec((1, rows, tqs), lambda b, hg, qi: (b, 3 * nw + hg, qi)),
            pl.BlockSpec((1, g_heads, XROWS, tqs), lambda b, hg, qi: (b, hg, 0, qi)),
            pl.BlockSpec((1, g_heads, XROWS, s), lambda b, hg, qi: (b, hg, 0, 0)),
        ],
        out_specs=pl.BlockSpec((1, rows, tqs), lambda b, hg, qi: (b, hg, qi)),
        out_shape=jax.ShapeDtypeStruct((bsz, n_heads * d, s), BF16),
        scratch_shapes=[
            pltpu.VMEM((n_sub, g_heads, d + XROWS, tq), BF16),
            pltpu.VMEM((g_heads, tk, tq), F32),
            pltpu.VMEM((g_heads, tk, tq), F32),
            pltpu.VMEM((g_heads, 1, tq), F32),
            pltpu.VMEM((g_heads, 1, tq), F32),
            pltpu.VMEM((n_sub, g_heads, 1, tq), F32),
            pltpu.VMEM((n_sub, g_heads, d + XROWS, tq), F32),
        ],
        compiler_params=pltpu.CompilerParams(
            dimension_semantics=("parallel", "parallel", "arbitrary"),
            vmem_limit_bytes=_vmem_limit(blocks, scratch, 4 * g_heads * tq * tk * 4),
        ),
        name="fox_attention",
    )(pt, pt, pt, pt, qx, kx)


def _swa_attn_kernel(q_ref, k_ref, v_ref, gate_ref, sink_ref, mask_ref, pick_ref, o_ref, *, blk, nsub):
    n = pl.program_id(2)
    d = HEAD_DIM
    g = q_ref.shape[1] // d
    ones_rows = (lax.broadcasted_iota(jnp.int32, (XROWS, 2 * blk), 0) == 0).astype(BF16)
    sink = sink_ref[0]

    def window_start(i):
        return pl.multiple_of(jnp.maximum(n * nsub + i - 1, 0) * blk, blk)

    scores = []
    for i in range(nsub):
        mask = jnp.where(n == 0, mask_ref[0], mask_ref[1]) if i == 0 else mask_ref[1]
        ka = jnp.concatenate([k_ref[0, :, pl.ds(window_start(i), 2 * blk)], mask], axis=0)
        qa = jnp.concatenate([q_ref[0, h * d:(h + 1) * d, i * blk:(i + 1) * blk] for h in range(g)], axis=1)
        qa = jnp.concatenate([qa, pick_ref[...]], axis=0)
        scores.append(lax.dot_general(ka, qa, (((0,), (0,)), ((), ())), preferred_element_type=F32))

    for i in range(nsub):
        start = window_start(i)
        s = scores[i]
        m = jnp.maximum(jnp.max(s, axis=0, keepdims=True), sink)
        e = jnp.exp2(s - m).astype(BF16)
        va = jnp.concatenate([v_ref[0, :, pl.ds(start, 2 * blk)], ones_rows], axis=0)
        o = jnp.dot(va, e, preferred_element_type=F32)
        denom = o[d:d + 1] + jnp.exp2(sink - m)
        y = o[0:d] * (1.0 / denom)
        for h in range(g):
            gate = gate_ref[0, h * d:(h + 1) * d, i * blk:(i + 1) * blk]
            o_ref[0, h * d:(h + 1) * d, i * blk:(i + 1) * blk] = _gated(y[:, h * blk:(h + 1) * blk], gate)


def _band_tables(blk):
    t = jnp.arange(blk)[:, None]
    j = jnp.arange(2 * blk)[None, :]

    def mask(first):
        diff = t - j + (0 if first else blk)
        return jnp.where((diff >= 0) & (diff < SWA_WINDOW), 0.0, NEG_INF)

    mask_t = jnp.stack([mask(True), mask(False)]).astype(BF16)
    pick = jnp.tile(jnp.eye(blk, dtype=BF16), (1, SWA_GROUP))
    return mask_t, pick


def _swa_attention(pt, sink_rows, *, blk, nsub):
    bsz, _, s = pt.shape
    d = HEAD_DIM
    rows = SWA_GROUP * d
    wq, wk = SWA_Q_HEADS * d, SWA_KV_HEADS * d
    tq = nsub * blk
    k0, v0, g0 = wq // d, (wq + wk) // d, (wq + 2 * wk) // rows
    mask_t, pick = _band_tables(blk)
    blocks = 3 * rows * tq * 2 + 2 * d * s * 2 + 8 * SWA_GROUP * blk * 4 + 4 * blk * blk * 2 + blk * rows * 2
    return pl.pallas_call(
        functools.partial(_swa_attn_kernel, blk=blk, nsub=nsub),
        grid=(bsz, SWA_KV_HEADS, s // tq),
        in_specs=[
            pl.BlockSpec((1, rows, tq), lambda b, h, n: (b, h, n)),
            pl.BlockSpec((1, d, s), lambda b, h, n: (b, k0 + h, 0)),
            pl.BlockSpec((1, d, s), lambda b, h, n: (b, v0 + h, 0)),
            pl.BlockSpec((1, rows, tq), lambda b, h, n: (b, g0 + h, n)),
            pl.BlockSpec((1, 1, SWA_GROUP * blk), lambda b, h, n: (h, 0, 0)),
            pl.BlockSpec((2, blk, 2 * blk), lambda b, h, n: (0, 0, 0)),
            pl.BlockSpec((blk, SWA_GROUP * blk), lambda b, h, n: (0, 0)),
        ],
        out_specs=pl.BlockSpec((1, rows, tq), lambda b, h, n: (b, h, n)),
        out_shape=jax.ShapeDtypeStruct((bsz, wq, s), BF16),
        compiler_params=pltpu.CompilerParams(
            dimension_semantics=("parallel", "parallel", "arbitrary"),
            vmem_limit_bytes=_vmem_limit(blocks, 0, 8 * nsub * 2 * blk * SWA_GROUP * blk * 4),
        ),
        name="swa_attention",
    )(pt, pt, pt, pt, sink_rows, mask_t, pick)


def _out_proj_kernel(zt_ref, x_ref, w_ref, fg_ref, o_ref, *, final_norm):
    out = x_ref[...] + lax.dot_general(zt_ref[0], w_ref[...], (((0,), (0,)), ((), ())),
                                       preferred_element_type=F32)
    if final_norm:
        ms = jnp.mean(out * out, axis=-1, keepdims=True)
        out = (out * lax.rsqrt(ms + RMS_EPS)) * fg_ref[...]
    o_ref[...] = out


def _out_proj(zt, x2d, w, fg, *, tm, final_norm):
    bsz, wid, s = zt.shape
    m, d = x2d.shape
    sb = s // tm
    blocks = wid * tm * 2 + 2 * tm * d * 4 + wid * d * 2 + d * 4
    return pl.pallas_call(
        functools.partial(_out_proj_kernel, final_norm=final_norm),
        grid=(m // tm,),
        in_specs=[
            pl.BlockSpec((1, wid, tm), lambda i: (i // sb, 0, i % sb)),
            pl.BlockSpec((tm, d), lambda i: (i, 0)),
            pl.BlockSpec((wid, d), lambda i: (0, 0)),
            pl.BlockSpec((1, d), lambda i: (0, 0)),
        ],
        out_specs=pl.BlockSpec((tm, d), lambda i: (i, 0)),
        out_shape=jax.ShapeDtypeStruct((m, d), F32),
        compiler_params=pltpu.CompilerParams(
            dimension_semantics=("parallel",),
            vmem_limit_bytes=_vmem_limit(blocks, 0, wid * tm * 2 + 3 * tm * d * 4),
        ),
        name="out_proj",
    )(zt, x2d, w, fg)


def _rope_tables(s):
    inv_freq = ROPE_THETA ** (-jnp.arange(ROT_HALF, dtype=F32) / ROT_HALF)
    ang = jnp.arange(s, dtype=F32)[:, None] * inv_freq[None, :]
    return jnp.cos(ang).T, jnp.sin(ang).T


PROJ_TM = 1024
FOX_PROJ_TN = 2048
SWA_PROJ_TN = 1536
WT_TN = 256
OUT_TM = 512
FOX_G, FOX_TQ, FOX_TK = 8, 512, 256
FOX_NSUB = 4
SWA_NSUB = 16


def kernel(x, norm_g, fox_w_in, fox_b_f, fox_w_out, swa_w_in, swa_sinks, swa_w_out, final_g):
    bsz, s, d = x.shape
    x2d = x.reshape(bsz * s, d)

    wid = FOX_HEADS * HEAD_DIM
    wt = fox_w_in[0].T.astype(BF16)
    pt, zt = _norm_proj(x2d, norm_g[0][None, :], wt, bsz=bsz, mode="fox", n_forget=FOX_HEADS,
                        tm=PROJ_TM, tn=FOX_PROJ_TN, q_blocks=wid // FOX_PROJ_TN)
    qx, kx = _decay_rows(zt, fox_b_f[0][:, None], FOX_HEADS)
    yt = _fox_attention(pt, qx, kx, n_heads=FOX_HEADS, g_heads=FOX_G, tq=FOX_TQ, tk=FOX_TK, n_sub=FOX_NSUB)
    x2d = _out_proj(yt, x2d, fox_w_out[0].astype(BF16), final_g[None, :], tm=OUT_TM, final_norm=False)

    wq = SWA_Q_HEADS * HEAD_DIM
    n_swa = swa_w_in.shape[2]
    wt = _transpose_cast(swa_w_in[0], n_swa, WT_TN)
    head = jnp.arange(n_swa // HEAD_DIM)
    head_scale = jnp.where(head < SWA_Q_HEADS, QK_SCALE, 1.0).astype(F32)
    head_rope = (head < SWA_Q_HEADS + SWA_KV_HEADS).astype(F32)
    (pt,) = _norm_proj(x2d, norm_g[1][None, :], wt, bsz=bsz, mode="swa", tm=PROJ_TM, tn=SWA_PROJ_TN,
                       tables=(head_scale, head_rope) + _rope_tables(s))
    sink_rows = jnp.repeat(swa_sinks[0].reshape(SWA_KV_HEADS, SWA_GROUP) * LOG2E, SWA_WINDOW, axis=1)[:, None, :]
    yt = _swa_attention(pt, sink_rows, blk=SWA_WINDOW, nsub=SWA_NSUB)
    out = _out_proj(yt, x2d, swa_w_out[0].astype(BF16), final_g[None, :], tm=OUT_TM, final_norm=True)
    return out.reshape(bsz, s, d)
```

```python
import functools

import jax
import jax.numpy as jnp
from jax import lax
from jax.experimental import pallas as pl
from jax.experimental.pallas import tpu as pltpu

F32 = jnp.float32
BF16 = jnp.bfloat16

RMS_EPS = 1e-6
NEG_INF = -1e30
LOG2E = 1.4426950408889634
HEAD_DIM = 64
FOX_HEADS = 32
SWA_Q_HEADS = 32
SWA_KV_HEADS = 4
SWA_GROUP = SWA_Q_HEADS // SWA_KV_HEADS
SWA_WINDOW = 128
ROPE_THETA = 500000.0
ROT_DIM = HEAD_DIM // 4
ROT_HALF = ROT_DIM // 2
XROWS = 16
LANES = 128
VMEM_CAPACITY = 64 * 1024 * 1024
QK_SCALE = HEAD_DIM ** -0.5 * LOG2E


def _gated(y, gate):
    half = 0.5 * gate.astype(F32)
    return (y * (half * (1.0 + jnp.tanh(half)))).astype(gate.dtype)


def _vmem_limit(pipelined_block_bytes, scratch_bytes, temp_bytes):
    need = 2 * pipelined_block_bytes + scratch_bytes + temp_bytes
    return int(min(need, VMEM_CAPACITY - (4 << 20)))


_NT = (((1,), (1,)), ((), ()))


def _norm_to_scratch(x_ref, g_ref, h_ref):
    x = x_ref[...]
    ms = jnp.mean(x * x, axis=-1, keepdims=True)
    h_ref[...] = ((x * lax.rsqrt(ms + RMS_EPS)) * g_ref[...]).astype(BF16)


def _norm_proj_fox_kernel(x_ref, g_ref, wt_ref, wft_ref, o_ref, of_ref, h_ref, *, q_blocks):
    j = pl.program_id(1)

    @pl.when(j == 0)
    def _():
        _norm_to_scratch(x_ref, g_ref, h_ref)
        of_ref[0] = lax.dot_general(wft_ref[...], h_ref[...], _NT, preferred_element_type=F32)

    res = lax.dot_general(wt_ref[...], h_ref[...], _NT, preferred_element_type=F32)
    o_ref[0] = (res * jnp.where(j < q_blocks, QK_SCALE, 1.0)).astype(o_ref.dtype)


def _norm_proj_swa_kernel(scale_ref, rope_ref, x_ref, g_ref, wt_ref, cos_ref, sin_ref, o_ref, h_ref):
    j = pl.program_id(1)
    tn = o_ref.shape[1]
    heads = tn // HEAD_DIM

    @pl.when(j == 0)
    def _():
        _norm_to_scratch(x_ref, g_ref, h_ref)

    res = lax.dot_general(wt_ref[...], h_ref[...], _NT, preferred_element_type=F32)
    cos, sin = cos_ref[...], sin_ref[...]
    for hh in range(heads):
        base = hh * HEAD_DIM
        scale = scale_ref[j * heads + hh]
        rope = rope_ref[j * heads + hh]
        ch = (1.0 + rope * (cos - 1.0)) * scale
        sh = (rope * sin) * scale
        x1 = res[base:base + ROT_HALF]
        x2 = res[base + ROT_HALF:base + ROT_DIM]
        o_ref[0, base:base + ROT_DIM, :] = jnp.concatenate(
            [x1 * ch - x2 * sh, x2 * ch + x1 * sh], axis=0).astype(o_ref.dtype)
        o_ref[0, base + ROT_DIM:base + HEAD_DIM, :] = (res[base + ROT_DIM:base + HEAD_DIM] * scale).astype(o_ref.dtype)


def _transpose_cast_kernel(w_ref, o_ref):
    o_ref[...] = w_ref[...].T.astype(o_ref.dtype)


def _transpose_cast(w, n, tn):
    d = w.shape[0]
    return pl.pallas_call(
        _transpose_cast_kernel,
        grid=(n // tn,),
        in_specs=[pl.BlockSpec((d, tn), lambda j: (0, j))],
        out_specs=pl.BlockSpec((tn, d), lambda j: (j, 0)),
        out_shape=jax.ShapeDtypeStruct((n, d), BF16),
        compiler_params=pltpu.CompilerParams(
            dimension_semantics=("parallel",),
            vmem_limit_bytes=_vmem_limit(d * tn * 4 + tn * d * 2, 0, 2 * d * tn * 4),
        ),
        name="transpose_cast",
    )(w)


def _norm_proj(x2d, g, wt, *, bsz, mode, tm, tn, q_blocks=None, n_forget=0, tables=None):
    m, d = x2d.shape
    n = wt.shape[0] - n_forget
    s = m // bsz
    sb = s // tm
    in_specs = [
        pl.BlockSpec((tm, d), lambda i, j: (i, 0)),
        pl.BlockSpec((1, d), lambda i, j: (0, 0)),
        pl.BlockSpec((tn, d), lambda i, j: (j, 0)),
    ]
    args = [x2d, g, wt]
    out_specs = [pl.BlockSpec((1, tn, tm), lambda i, j: (i // sb, j, i % sb))]
    out_shape = [jax.ShapeDtypeStruct((bsz, n, s), BF16)]
    blocks = tm * d * 4 + d * 4 + tn * d * 2 + tn * tm * 2
    if mode == "fox":
        body = functools.partial(_norm_proj_fox_kernel, q_blocks=q_blocks)
        in_specs.append(pl.BlockSpec((n_forget, d), lambda i, j: (n // n_forget, 0)))
        args.append(wt)
        out_specs.append(pl.BlockSpec((1, n_forget, tm), lambda i, j: (i // sb, 0, i % sb)))
        out_shape.append(jax.ShapeDtypeStruct((bsz, n_forget, s), F32))
        blocks += n_forget * d * 2 + n_forget * tm * 4
    else:
        body = _norm_proj_swa_kernel
        scale, rope, cos, sin = tables
        in_specs = [pl.BlockSpec(memory_space=pltpu.SMEM)] * 2 + in_specs
        in_specs += [pl.BlockSpec((ROT_HALF, tm), lambda i, j: (0, i % sb))] * 2
        args = [scale, rope] + args + [cos, sin]
        blocks += 2 * ROT_HALF * tm * 4
    return pl.pallas_call(
        body,
        grid=(m // tm, n // tn),
        in_specs=in_specs,
        out_specs=out_specs,
        out_shape=out_shape,
        scratch_shapes=[pltpu.VMEM((tm, d), BF16)],
        compiler_params=pltpu.CompilerParams(
            dimension_semantics=("parallel", "arbitrary"),
            vmem_limit_bytes=_vmem_limit(blocks, tm * d * 2, 3 * tm * d * 4 + 2 * tn * tm * 4),
        ),
        name="norm_proj_" + mode,
    )(*args)


def _decay_rows_kernel(z_ref, b_ref, qx_ref, kx_ref):
    z = z_ref[0] + b_ref[...]
    x = (jnp.minimum(z, 0.0) - jnp.log1p(jnp.exp(-jnp.abs(z)))) * LOG2E
    n_heads, s = x.shape
    col = lax.broadcasted_iota(jnp.int32, x.shape, 1)
    shift = 1
    while shift < s:
        x = x + jnp.where(col >= shift, pltpu.roll(x, shift, axis=1), 0.0)
        shift *= 2
    hi = x.astype(BF16).astype(F32)
    mid = (x - hi).astype(BF16).astype(F32)
    lo = (x - hi - mid).astype(BF16).astype(F32)
    row = lax.broadcasted_iota(jnp.int32, (XROWS, s), 0)
    for h in range(n_heads):
        parts_q = jnp.where(row == 0, hi[h:h + 1], jnp.where(row == 1, mid[h:h + 1], lo[h:h + 1]))
        parts_k = jnp.where(row == 3, hi[h:h + 1], jnp.where(row == 4, mid[h:h + 1], lo[h:h + 1]))
        qx = jnp.where(row < 3, parts_q, jnp.where(row < 6, 1.0, 0.0))
        kx = jnp.where(row < 3, 1.0, jnp.where(row < 6, -parts_k, 0.0))
        qx_ref[0, h] = qx.astype(BF16)
        kx_ref[0, h] = kx.astype(BF16)


def _decay_rows(z, b, n_heads):
    bsz, _, s = z.shape
    out_blk = n_heads * XROWS * s * 2
    return pl.pallas_call(
        _decay_rows_kernel,
        grid=(bsz,),
        in_specs=[
            pl.BlockSpec((1, n_heads, s), lambda i: (i, 0, 0)),
            pl.BlockSpec((n_heads, 1), lambda i: (0, 0)),
        ],
        out_specs=[pl.BlockSpec((1, n_heads, XROWS, s), lambda i: (i, 0, 0, 0))] * 2,
        out_shape=[jax.ShapeDtypeStruct((bsz, n_heads, XROWS, s), BF16)] * 2,
        compiler_params=pltpu.CompilerParams(
            dimension_semantics=("parallel",),
            vmem_limit_bytes=_vmem_limit(n_heads * s * 4 + 2 * out_blk, 0, 8 * n_heads * s * 4),
        ),
        name="decay_rows",
    )(z, b)


def _fox_attn_kernel(q_ref, k_ref, v_ref, gate_ref, qx_ref, kx_ref, o_ref,
                     qa_ref, s0_ref, s1_ref, bm0_ref, bm1_ref, m_ref, acc_ref, *, g_heads, tq, tk, n_sub):
    tiles = []
    for sub in range(n_sub):
        cols = slice(sub * tq, (sub + 1) * tq)
        tiles.append(_fox_query_tile(pl.program_id(2) * n_sub + sub,
                                     q_ref.at[:, :, cols], k_ref, v_ref, gate_ref.at[:, :, cols],
                                     qx_ref.at[:, :, :, cols], kx_ref, o_ref.at[:, :, cols],
                                     qa_ref.at[sub], s0_ref, s1_ref, bm0_ref, bm1_ref, m_ref.at[sub],
                                     acc_ref.at[sub], g_heads=g_heads, tq=tq, tk=tk))
    for init, _, _ in tiles:
        init()
    for g in range(g_heads):
        tiles[0][1](g)
    for sub, (_, _, run) in enumerate(tiles):
        run(tiles[sub + 1][1] if sub + 1 < n_sub else None)


def _fox_query_tile(qi, q_ref, k_ref, v_ref, gate_ref, qx_ref, kx_ref, o_ref,
                    qa_ref, s0_ref, s1_ref, bm0_ref, bm1_ref, m_ref, acc_ref, *, g_heads, tq, tk):
    n_diag = tq // tk
    n_full = qi * n_diag
    d = HEAD_DIM
    s_refs = (s0_ref, s1_ref)
    bm_refs = (bm0_ref, bm1_ref)

    def init():
        for g in range(g_heads):
            qa_ref[g, 0:d, :] = q_ref[0, g * d:(g + 1) * d, :]
            qa_ref[g, d:d + XROWS, :] = qx_ref[0, g]
        m_ref[...] = jnp.full(m_ref.shape, NEG_INF, F32)
        acc_ref[...] = jnp.zeros(acc_ref.shape, F32)

    ones_rows = (lax.broadcasted_iota(jnp.int32, (XROWS, tk), 0) == 0).astype(BF16)

    def scores(j, slot, g, col0=0):
        off = pl.multiple_of(j * tk, tk)
        ka = jnp.concatenate([k_ref[0, g * d:(g + 1) * d, pl.ds(off, tk)],
                              kx_ref[0, g, :, pl.ds(off, tk)]], axis=0)
        s = lax.dot_general(ka, qa_ref[g, :, col0:], (((0,), (0,)), ((), ())),
                            preferred_element_type=F32)
        s_refs[slot][g, :, col0:] = s
        bm_refs[slot][g, :, col0:] = jnp.max(s, axis=0, keepdims=True)

    def softmax_pv(j, slot, g, diag):
        col0 = 0 if diag is None else diag * tk
        off = pl.multiple_of(j * tk, tk)
        s = s_refs[slot][g, :, col0:]
        if diag is None:
            bm = bm_refs[slot][g]
        else:
            krow = lax.broadcasted_iota(jnp.int32, s.shape, 0)
            qcol = lax.broadcasted_iota(jnp.int32, s.shape, 1)
            s = jnp.where(krow <= qcol, s, NEG_INF)
            bm = jnp.max(s, axis=0, keepdims=True)
        m_prev = m_ref[g, :, col0:]
        m_new = jnp.maximum(m_prev, bm)
        alpha = jnp.exp2(m_prev - m_new)
        p = jnp.exp2(s - m_new).astype(BF16)
        va = jnp.concatenate([v_ref[0, g * d:(g + 1) * d, pl.ds(off, tk)], ones_rows], axis=0)
        acc_ref[g, :, col0:] = alpha * acc_ref[g, :, col0:] + jnp.dot(va, p, preferred_element_type=F32)
        m_ref[g, :, col0:] = m_new

    def step(j, slot, diag, next_col0=0):
        for g in range(g_heads):
            if next_col0 is not None:
                scores(j + 1, 1 - slot, g, next_col0)
            softmax_pv(j, slot, g, diag)

    def first_scores(g):
        scores(0, 0, g)

    def steps(j0, count):
        for u in range(count):
            step(j0 + u, u % 2, None)

    def run(next_first_scores):
        assert n_diag == 2
        rem = n_full % 4

        @pl.when(rem == 2)
        def _():
            steps(0, 2)

        def four_steps(jj, carry):
            steps(rem + 4 * jj, 4)
            return carry

        lax.fori_loop(0, n_full // 4, four_steps, 0)
        step(n_full, 0, 0, next_col0=tk)
        for g in range(g_heads):
            if next_first_scores is not None:
                next_first_scores(g)
            softmax_pv(n_full + 1, 1, g, 1)

        for g in range(g_heads):
            y = acc_ref[g, 0:d, :] * (1.0 / acc_ref[g, d:d + 1, :])
            o_ref[0, g * d:(g + 1) * d, :] = _gated(y, gate_ref[0, g * d:(g + 1) * d, :])

    return init, first_scores, run


def _fox_attention(pt, qx, kx, *, n_heads, g_heads, tq, tk, n_sub):
    bsz, _, s = pt.shape
    d = HEAD_DIM
    rows = g_heads * d
    nw = n_heads // g_heads
    tqs = n_sub * tq
    blocks = 3 * rows * tqs * 2 + 2 * rows * s * 2 + g_heads * XROWS * (tqs + s) * 2
    scratch = g_heads * (n_sub * ((d + XROWS) * tq * 2 + 8 * tq * 4 + (d + XROWS) * tq * 4)
                         + 2 * 8 * tq * 4 + 2 * tk * tq * 4)
    return pl.pallas_call(
        functools.partial(_fox_attn_kernel, g_heads=g_heads, tq=tq, tk=tk, n_sub=n_sub),
        grid=(bsz, nw, s // tqs),
        in_specs=[
            pl.BlockSpec((1, rows, tqs), lambda b, hg, qi: (b, hg, qi)),
            pl.BlockSpec((1, rows, s), lambda b, hg, qi: (b, nw + hg, 0)),
            pl.BlockSpec((1, rows, s), lambda b, hg, qi: (b, 2 * nw + hg, 0)),
            pl.BlockSpec((1, rows, tqs), lambda b, hg, qi: (b, 3 * nw + hg, qi)),
            pl.BlockSpec((1, g_heads, XROWS, tqs), lambda b, hg, qi: (b, hg, 0, qi)),
            pl.BlockSpec((1, g_heads, XROWS, s), lambda b, hg, qi: (b, hg, 0, 0)),
        ],
        out_specs=pl.BlockSpec((1, rows, tqs), lambda b, hg, qi: (b, hg, qi)),
        out_shape=jax.ShapeDtypeStruct((bsz, n_heads * d, s), BF16),
        scratch_shapes=[
            pltpu.VMEM((n_sub, g_heads, d + XROWS, tq), BF16),
            pltpu.VMEM((g_heads, tk, tq), F32),
            pltpu.VMEM((g_heads, tk, tq), F32),
            pltpu.VMEM((g_heads, 1, tq), F32),
            pltpu.VMEM((g_heads, 1, tq), F32),
            pltpu.VMEM((n_sub, g_heads, 1, tq), F32),
            pltpu.VMEM((n_sub, g_heads, d + XROWS, tq), F32),
        ],
        compiler_params=pltpu.CompilerParams(
            dimension_semantics=("parallel", "parallel", "arbitrary"),
            vmem_limit_bytes=_vmem_limit(blocks, scratch, 4 * g_heads * tq * tk * 4),
        ),
        name="fox_attention",
    )(pt, pt, pt, pt, qx, kx)


def _swa_attn_kernel(q_ref, k_ref, v_ref, gate_ref, sink_ref, mask_ref, pick_ref, o_ref, *, blk, nsub):
    n = pl.program_id(2)
    d = HEAD_DIM
    g = q_ref.shape[1] // d
    ones_rows = (lax.broadcasted_iota(jnp.int32, (XROWS, 2 * blk), 0) == 0).astype(BF16)
    sink = sink_ref[0]

    def window_start(i):
        return pl.multiple_of(jnp.maximum(n * nsub + i - 1, 0) * blk, blk)

    scores = []
    for i in range(nsub):
        mask = jnp.where(n == 0, mask_ref[0], mask_ref[1]) if i == 0 else mask_ref[1]
        ka = jnp.concatenate([k_ref[0, :, pl.ds(window_start(i), 2 * blk)], mask], axis=0)
        qa = jnp.concatenate([q_ref[0, h * d:(h + 1) * d, i * blk:(i + 1) * blk] for h in range(g)], axis=1)
        qa = jnp.concatenate([qa, pick_ref[...]], axis=0)
        scores.append(lax.dot_general(ka, qa, (((0,), (0,)), ((), ())), preferred_element_type=F32))

    for i in range(nsub):
        start = window_start(i)
        s = scores[i]
        m = jnp.maximum(jnp.max(s, axis=0, keepdims=True), sink)
        e = jnp.exp2(s - m).astype(BF16)
        va = jnp.concatenate([v_ref[0, :, pl.ds(start, 2 * blk)], ones_rows], axis=0)
        o = jnp.dot(va, e, preferred_element_type=F32)
        denom = o[d:d + 1] + jnp.exp2(sink - m)
        y = o[0:d] * (1.0 / denom)
        for h in range(g):
            gate = gate_ref[0, h * d:(h + 1) * d, i * blk:(i + 1) * blk]
            o_ref[0, h * d:(h + 1) * d, i * blk:(i + 1) * blk] = _gated(y[:, h * blk:(h + 1) * blk], gate)


def _band_tables(blk):
    t = jnp.arange(blk)[:, None]
    j = jnp.arange(2 * blk)[None, :]

    def mask(first):
        diff = t - j + (0 if first else blk)
        return jnp.where((diff >= 0) & (diff < SWA_WINDOW), 0.0, NEG_INF)

    mask_t = jnp.stack([mask(True), mask(False)]).astype(BF16)
    pick = jnp.tile(jnp.eye(blk, dtype=BF16), (1, SWA_GROUP))
    return mask_t, pick


def _swa_attention(pt, sink_rows, *, blk, nsub):
    bsz, _, s = pt.shape
    d = HEAD_DIM
    rows = SWA_GROUP * d
    wq, wk = SWA_Q_HEADS * d, SWA_KV_HEADS * d
    tq = nsub * blk
    k0, v0, g0 = wq // d, (wq + wk) // d, (wq + 2 * wk) // rows
    mask_t, pick = _band_tables(blk)
    blocks = 3 * rows * tq * 2 + 2 * d * s * 2 + 8 * SWA_GROUP * blk * 4 + 4 * blk * blk * 2 + blk * rows * 2
    return pl.pallas_call(
        functools.partial(_swa_attn_kernel, blk=blk, nsub=nsub),
        grid=(bsz, SWA_KV_HEADS, s // tq),
        in_specs=[
            pl.BlockSpec((1, rows, tq), lambda b, h, n: (b, h, n)),
            pl.BlockSpec((1, d, s), lambda b, h, n: (b, k0 + h, 0)),
            pl.BlockSpec((1, d, s), lambda b, h, n: (b, v0 + h, 0)),
            pl.BlockSpec((1, rows, tq), lambda b, h, n: (b, g0 + h, n)),
            pl.BlockSpec((1, 1, SWA_GROUP * blk), lambda b, h, n: (h, 0, 0)),
            pl.BlockSpec((2, blk, 2 * blk), lambda b, h, n: (0, 0, 0)),
            pl.BlockSpec((blk, SWA_GROUP * blk), lambda b, h, n: (0, 0)),
        ],
        out_specs=pl.BlockSpec((1, rows, tq), lambda b, h, n: (b, h, n)),
        out_shape=jax.ShapeDtypeStruct((bsz, wq, s), BF16),
        compiler_params=pltpu.CompilerParams(
            dimension_semantics=("parallel", "parallel", "arbitrary"),
            vmem_limit_bytes=_vmem_limit(blocks, 0, 8 * nsub * 2 * blk * SWA_GROUP * blk * 4),
        ),
        name="swa_attention",
    )(pt, pt, pt, pt, sink_rows, mask_t, pick)


def _out_proj_kernel(zt_ref, x_ref, w_ref, fg_ref, o_ref, *, final_norm):
    out = x_ref[...] + lax.dot_general(zt_ref[0], w_ref[...], (((0,), (0,)), ((), ())),
                                       preferred_element_type=F32)
    if final_norm:
        ms = jnp.mean(out * out, axis=-1, keepdims=True)
        out = (out * lax.rsqrt(ms + RMS_EPS)) * fg_ref[...]
    o_ref[...] = out


def _out_proj(zt, x2d, w, fg, *, tm, final_norm):
    bsz, wid, s = zt.shape
    m, d = x2d.shape
    sb = s // tm
    blocks = wid * tm * 2 + 2 * tm * d * 4 + wid * d * 2 + d * 4
    return pl.pallas_call(
        functools.partial(_out_proj_kernel, final_norm=final_norm),
        grid=(m // tm,),
        in_specs=[
            pl.BlockSpec((1, wid, tm), lambda i: (i // sb, 0, i % sb)),
            pl.BlockSpec((tm, d), lambda i: (i, 0)),
            pl.BlockSpec((wid, d), lambda i: (0, 0)),
            pl.BlockSpec((1, d), lambda i: (0, 0)),
        ],
        out_specs=pl.BlockSpec((tm, d), lambda i: (i, 0)),
        out_shape=jax.ShapeDtypeStruct((m, d), F32),
        compiler_params=pltpu.CompilerParams(
            dimension_semantics=("parallel",),
            vmem_limit_bytes=_vmem_limit(blocks, 0, wid * tm * 2 + 3 * tm * d * 4),
        ),
        name="out_proj",
    )(zt, x2d, w, fg)


def _rope_tables(s):
    inv_freq = ROPE_THETA ** (-jnp.arange(ROT_HALF, dtype=F32) / ROT_HALF)
    ang = jnp.arange(s, dtype=F32)[:, None] * inv_freq[None, :]
    return jnp.cos(ang).T, jnp.sin(ang).T


PROJ_TM = 1024
FOX_PROJ_TN = 2048
SWA_PROJ_TN = 1536
WT_TN = 256
OUT_TM = 512
FOX_G, FOX_TQ, FOX_TK = 8, 512, 256
FOX_NSUB = 4
SWA_NSUB = 16


def kernel(x, norm_g, fox_w_in, fox_b_f, fox_w_out, swa_w_in, swa_sinks, swa_w_out, final_g):
    bsz, s, d = x.shape
    x2d = x.reshape(bsz * s, d)

    wid = FOX_HEADS * HEAD_DIM
    wt = fox_w_in[0].T.astype(BF16)
    pt, zt = _norm_proj(x2d, norm_g[0][None, :], wt, bsz=bsz, mode="fox", n_forget=FOX_HEADS,
                        tm=PROJ_TM, tn=FOX_PROJ_TN, q_blocks=wid // FOX_PROJ_TN)
    qx, kx = _decay_rows(zt, fox_b_f[0][:, None], FOX_HEADS)
    yt = _fox_attention(pt, qx, kx, n_heads=FOX_HEADS, g_heads=FOX_G, tq=FOX_TQ, tk=FOX_TK, n_sub=FOX_NSUB)
    x2d = _out_proj(yt, x2d, fox_w_out[0].astype(BF16), final_g[None, :], tm=OUT_TM, final_norm=False)

    wq = SWA_Q_HEADS * HEAD_DIM
    n_swa = swa_w_in.shape[2]
    wt = _transpose_cast(swa_w_in[0], n_swa, WT_TN)
    head = jnp.arange(n_swa // HEAD_DIM)
    head_scale = jnp.where(head < SWA_Q_HEADS, QK_SCALE, 1.0).astype(F32)
    head_rope = (head < SWA_Q_HEADS + SWA_KV_HEADS).astype(F32)
    (pt,) = _norm_proj(x2d, norm_g[1][None, :], wt, bsz=bsz, mode="swa", tm=PROJ_TM, tn=SWA_PROJ_TN,
                       tables=(head_scale, head_rope) + _rope_tables(s))
    sink_rows = jnp.repeat(swa_sinks[0].reshape(SWA_KV_HEADS, SWA_GROUP) * LOG2E, SWA_WINDOW, axis=1)[:, None, :]
    yt = _swa_attention(pt, sink_rows, blk=SWA_WINDOW, nsub=SWA_NSUB)
    out = _out_proj(yt, x2d, swa_w_out[0].astype(BF16), final_g[None, :], tm=OUT_TM, final_norm=True)
    return out.reshape(bsz, s, d)
```

```python
import functools

import jax
import jax.numpy as jnp
from jax import lax
from jax.experimental import pallas as pl
from jax.experimental.pallas import tpu as pltpu

F32 = jnp.float32
BF16 = jnp.bfloat16

RMS_EPS = 1e-6
NEG_INF = -1e30
LOG2E = 1.4426950408889634
HEAD_DIM = 64
FOX_HEADS = 32
SWA_Q_HEADS = 32
SWA_KV_HEADS = 4
SWA_GROUP = SWA_Q_HEADS // SWA_KV_HEADS
SWA_WINDOW = 128
ROPE_THETA = 500000.0
ROT_DIM = HEAD_DIM // 4
ROT_HALF = ROT_DIM // 2
XROWS = 16
LANES = 128
VMEM_CAPACITY = 64 * 1024 * 1024
QK_SCALE = HEAD_DIM ** -0.5 * LOG2E


def _gated(y, gate):
    half = 0.5 * gate.astype(F32)
    return (y * (half * (1.0 + jnp.tanh(half)))).astype(gate.dtype)


def _vmem_limit(pipelined_block_bytes, scratch_bytes, temp_bytes):
    need = 2 * pipelined_block_bytes + scratch_bytes + temp_bytes
    return int(min(need, VMEM_CAPACITY - (4 << 20)))


_NT = (((1,), (1,)), ((), ()))


def _norm_to_scratch(x_ref, g_ref, h_ref):
    x = x_ref[...]
    ms = jnp.mean(x * x, axis=-1, keepdims=True)
    h_ref[...] = ((x * lax.rsqrt(ms + RMS_EPS)) * g_ref[...]).astype(BF16)


def _norm_proj_fox_kernel(x_ref, g_ref, wt_ref, wft_ref, o_ref, of_ref, h_ref, *, q_blocks):
    j = pl.program_id(1)

    @pl.when(j == 0)
    def _():
        _norm_to_scratch(x_ref, g_ref, h_ref)
        of_ref[0] = lax.dot_general(wft_ref[...], h_ref[...], _NT, preferred_element_type=F32)

    res = lax.dot_general(wt_ref[...], h_ref[...], _NT, preferred_element_type=F32)
    o_ref[0] = (res * jnp.where(j < q_blocks, QK_SCALE, 1.0)).astype(o_ref.dtype)


def _norm_proj_swa_kernel(scale_ref, rope_ref, x_ref, g_ref, wt_ref, cos_ref, sin_ref, o_ref, h_ref):
    j = pl.program_id(1)
    tn = o_ref.shape[1]
    heads = tn // HEAD_DIM

    @pl.when(j == 0)
    def _():
        _norm_to_scratch(x_ref, g_ref, h_ref)

    res = lax.dot_general(wt_ref[...], h_ref[...], _NT, preferred_element_type=F32)
    cos, sin = cos_ref[...], sin_ref[...]
    for hh in range(heads):
        base = hh * HEAD_DIM
        scale = scale_ref[j * heads + hh]
        rope = rope_ref[j * heads + hh]
        ch = (1.0 + rope * (cos - 1.0)) * scale
        sh = (rope * sin) * scale
        x1 = res[base:base + ROT_HALF]
        x2 = res[base + ROT_HALF:base + ROT_DIM]
        o_ref[0, base:base + ROT_DIM, :] = jnp.concatenate(
            [x1 * ch - x2 * sh, x2 * ch + x1 * sh], axis=0).astype(o_ref.dtype)
        o_ref[0, base + ROT_DIM:base + HEAD_DIM, :] = (res[base + ROT_DIM:base + HEAD_DIM] * scale).astype(o_ref.dtype)


def _transpose_cast_kernel(w_ref, o_ref):
    o_ref[...] = w_ref[...].T.astype(o_ref.dtype)


def _transpose_cast(w, n, tn):
    d = w.shape[0]
    return pl.pallas_call(
        _transpose_cast_kernel,
        grid=(n // tn,),
        in_specs=[pl.BlockSpec((d, tn), lambda j: (0, j))],
        out_specs=pl.BlockSpec((tn, d), lambda j: (j, 0)),
        out_shape=jax.ShapeDtypeStruct((n, d), BF16),
        compiler_params=pltpu.CompilerParams(
            dimension_semantics=("parallel",),
            vmem_limit_bytes=_vmem_limit(d * tn * 4 + tn * d * 2, 0, 2 * d * tn * 4),
        ),
        name="transpose_cast",
    )(w)


def _norm_proj(x2d, g, wt, *, bsz, mode, tm, tn, q_blocks=None, n_forget=0, tables=None):
    m, d = x2d.shape
    n = wt.shape[0] - n_forget
    s = m // bsz
    sb = s // tm
    in_specs = [
        pl.BlockSpec((tm, d), lambda i, j: (i, 0)),
        pl.BlockSpec((1, d), lambda i, j: (0, 0)),
        pl.BlockSpec((tn, d), lambda i, j: (j, 0)),
    ]
    args = [x2d, g, wt]
    out_specs = [pl.BlockSpec((1, tn, tm), lambda i, j: (i // sb, j, i % sb))]
    out_shape = [jax.ShapeDtypeStruct((bsz, n, s), BF16)]
    blocks = tm * d * 4 + d * 4 + tn * d * 2 + tn * tm * 2
    if mode == "fox":
        body = functools.partial(_norm_proj_fox_kernel, q_blocks=q_blocks)
        in_specs.append(pl.BlockSpec((n_forget, d), lambda i, j: (n // n_forget, 0)))
        args.append(wt)
        out_specs.append(pl.BlockSpec((1, n_forget, tm), lambda i, j: (i // sb, 0, i % sb)))
        out_shape.append(jax.ShapeDtypeStruct((bsz, n_forget, s), F32))
        blocks += n_forget * d * 2 + n_forget * tm * 4
    else:
        body = _norm_proj_swa_kernel
        scale, rope, cos, sin = tables
        in_specs = [pl.BlockSpec(memory_space=pltpu.SMEM)] * 2 + in_specs
        in_specs += [pl.BlockSpec((ROT_HALF, tm), lambda i, j: (0, i % sb))] * 2
        args = [scale, rope] + args + [cos, sin]
        blocks += 2 * ROT_HALF * tm * 4
    return pl.pallas_call(
        body,
        grid=(m // tm, n // tn),
        in_specs=in_specs,
        out_specs=out_specs,
        out_shape=out_shape,
        scratch_shapes=[pltpu.VMEM((tm, d), BF16)],
        compiler_params=pltpu.CompilerParams(
            dimension_semantics=("parallel", "arbitrary"),
            vmem_limit_bytes=_vmem_limit(blocks, tm * d * 2, 3 * tm * d * 4 + 2 * tn * tm * 4),
        ),
        name="norm_proj_" + mode,
    )(*args)


def _decay_rows_kernel(z_ref, b_ref, qx_ref, kx_ref):
    z = z_ref[0] + b_ref[...]
    x = (jnp.minimum(z, 0.0) - jnp.log1p(jnp.exp(-jnp.abs(z)))) * LOG2E
    n_heads, s = x.shape
    col = lax.broadcasted_iota(jnp.int32, x.shape, 1)
    shift = 1
    while shift < s:
        x = x + jnp.where(col >= shift, pltpu.roll(x, shift, axis=1), 0.0)
        shift *= 2
    hi = x.astype(BF16).astype(F32)
    mid = (x - hi).astype(BF16).astype(F32)
    lo = (x - hi - mid).astype(BF16).astype(F32)
    row = lax.broadcasted_iota(jnp.int32, (XROWS, s), 0)
    for h in range(n_heads):
        parts_q = jnp.where(row == 0, hi[h:h + 1], jnp.where(row == 1, mid[h:h + 1], lo[h:h + 1]))
        parts_k = jnp.where(row == 3, hi[h:h + 1], jnp.where(row == 4, mid[h:h + 1], lo[h:h + 1]))
        qx = jnp.where(row < 3, parts_q, jnp.where(row < 6, 1.0, 0.0))
        kx = jnp.where(row < 3, 1.0, jnp.where(row < 6, -parts_k, 0.0))
        qx_ref[0, h] = qx.astype(BF16)
        kx_ref[0, h] = kx.astype(BF16)


def _decay_rows(z, b, n_heads):
    bsz, _, s = z.shape
    out_blk = n_heads * XROWS * s * 2
    return pl.pallas_call(
        _decay_rows_kernel,
        grid=(bsz,),
        in_specs=[
            pl.BlockSpec((1, n_heads, s), lambda i: (i, 0, 0)),
            pl.BlockSpec((n_heads, 1), lambda i: (0, 0)),
        ],
        out_specs=[pl.BlockSpec((1, n_heads, XROWS, s), lambda i: (i, 0, 0, 0))] * 2,
        out_shape=[jax.ShapeDtypeStruct((bsz, n_heads, XROWS, s), BF16)] * 2,
        compiler_params=pltpu.CompilerParams(
            dimension_semantics=("parallel",),
            vmem_limit_bytes=_vmem_limit(n_heads * s * 4 + 2 * out_blk, 0, 8 * n_heads * s * 4),
        ),
        name="decay_rows",
    )(z, b)


def _fox_attn_kernel(q_ref, k_ref, v_ref, gate_ref, qx_ref, kx_ref, o_ref,
                     qa_ref, s0_ref, s1_ref, bm0_ref, bm1_ref, m_ref, acc_ref, *, g_heads, tq, tk, n_sub):
    tiles = []
    for sub in range(n_sub):
        cols = slice(sub * tq, (sub + 1) * tq)
        tiles.append(_fox_query_tile(pl.program_id(2) * n_sub + sub,
                                     q_ref.at[:, :, cols], k_ref, v_ref, gate_ref.at[:, :, cols],
                                     qx_ref.at[:, :, :, cols], kx_ref, o_ref.at[:, :, cols],
                                     qa_ref.at[sub], s0_ref, s1_ref, bm0_ref, bm1_ref, m_ref.at[sub],
                                     acc_ref.at[sub], g_heads=g_heads, tq=tq, tk=tk))
    for init, _, _ in tiles:
        init()
    for g in range(g_heads):
        tiles[0][1](g)
    for sub, (_, _, run) in enumerate(tiles):
        run(tiles[sub + 1][1] if sub + 1 < n_sub else None)


def _fox_query_tile(qi, q_ref, k_ref, v_ref, gate_ref, qx_ref, kx_ref, o_ref,
                    qa_ref, s0_ref, s1_ref, bm0_ref, bm1_ref, m_ref, acc_ref, *, g_heads, tq, tk):
    n_diag = tq // tk
    n_full = qi * n_diag
    d = HEAD_DIM
    s_refs = (s0_ref, s1_ref)
    bm_refs = (bm0_ref, bm1_ref)

    def init():
        for g in range(g_heads):
            qa_ref[g, 0:d, :] = q_ref[0, g * d:(g + 1) * d, :]
            qa_ref[g, d:d + XROWS, :] = qx_ref[0, g]
        m_ref[...] = jnp.full(m_ref.shape, NEG_INF, F32)
        acc_ref[...] = jnp.zeros(acc_ref.shape, F32)

    ones_rows = (lax.broadcasted_iota(jnp.int32, (XROWS, tk), 0) == 0).astype(BF16)

    def scores(j, slot, g, col0=0):
        off = pl.multiple_of(j * tk, tk)
        ka = jnp.concatenate([k_ref[0, g * d:(g + 1) * d, pl.ds(off, tk)],
                              kx_ref[0, g, :, pl.ds(off, tk)]], axis=0)
        s = lax.dot_general(ka, qa_ref[g, :, col0:], (((0,), (0,)), ((), ())),
                            preferred_element_type=F32)
        s_refs[slot][g, :, col0:] = s
        bm_refs[slot][g, :, col0:] = jnp.max(s, axis=0, keepdims=True)

    def softmax_pv(j, slot, g, diag):
        col0 = 0 if diag is None else diag * tk
        off = pl.multiple_of(j * tk, tk)
        s = s_refs[slot][g, :, col0:]
        if diag is None:
            bm = bm_refs[slot][g]
        else:
            krow = lax.broadcasted_iota(jnp.int32, s.shape, 0)
            qcol = lax.broadcasted_iota(jnp.int32, s.shape, 1)
            s = jnp.where(krow <= qcol, s, NEG_INF)
            bm = jnp.max(s, axis=0, keepdims=True)
        m_prev = m_ref[g, :, col0:]
        m_new = jnp.maximum(m_prev, bm)
        alpha = jnp.exp2(m_prev - m_new)
        p = jnp.exp2(s - m_new).astype(BF16)
        va = jnp.concatenate([v_ref[0, g * d:(g + 1) * d, pl.ds(off, tk)], ones_rows], axis=0)
        acc_ref[g, :, col0:] = alpha * acc_ref[g, :, col0:] + jnp.dot(va, p, preferred_element_type=F32)
        m_ref[g, :, col0:] = m_new

    def step(j, slot, diag, next_col0=0):
        for g in range(g_heads):
            if next_col0 is not None:
                scores(j + 1, 1 - slot, g, next_col0)
            softmax_pv(j, slot, g, diag)

    def first_scores(g):
        scores(0, 0, g)

    def steps(j0, count):
        for u in range(count):
            step(j0 + u, u % 2, None)

    def run(next_first_scores):
        assert n_diag == 2
        rem = n_full % 4

        @pl.when(rem == 2)
        def _():
            steps(0, 2)

        def four_steps(jj, carry):
            steps(rem + 4 * jj, 4)
            return carry

        lax.fori_loop(0, n_full // 4, four_steps, 0)
        step(n_full, 0, 0, next_col0=tk)
        for g in range(g_heads):
            if next_first_scores is not None:
                next_first_scores(g)
            softmax_pv(n_full + 1, 1, g, 1)

        for g in range(g_heads):
            y = acc_ref[g, 0:d, :] * (1.0 / acc_ref[g, d:d + 1, :])
            o_ref[0, g * d:(g + 1) * d, :] = _gated(y, gate_ref[0, g * d:(g + 1) * d, :])

    return init, first_scores, run


def _fox_attention(pt, qx, kx, *, n_heads, g_heads, tq, tk, n_sub):
    bsz, _, s = pt.shape
    d = HEAD_DIM
    rows = g_heads * d
    nw = n_heads // g_heads
    tqs = n_sub * tq
    blocks = 3 * rows * tqs * 2 + 2 * rows * s * 2 + g_heads * XROWS * (tqs + s) * 2
    scratch = g_heads * (n_sub * ((d + XROWS) * tq * 2 + 8 * tq * 4 + (d + XROWS) * tq * 4)
                         + 2 * 8 * tq * 4 + 2 * tk * tq * 4)
    return pl.pallas_call(
        functools.partial(_fox_attn_kernel, g_heads=g_heads, tq=tq, tk=tk, n_sub=n_sub),
        grid=(bsz, nw, s // tqs),
        in_specs=[
            pl.BlockSpec((1, rows, tqs), lambda b, hg, qi: (b, hg, qi)),
            pl.BlockSpec((1, rows, s), lambda b, hg, qi: (b, nw + hg, 0)),
            pl.BlockSpec((1, rows, s), lambda b, hg, qi: (b, 2 * nw + hg, 0)),
            pl.BlockSpec((1, rows, tqs), lambda b, hg, qi: (b, 3 * nw + hg, qi)),
            pl.BlockSpec((1, g_heads, XROWS, tqs), lambda b, hg, qi: (b, hg, 0, qi)),
            pl.BlockSpec((1, g_heads, XROWS, s), lambda b, hg, qi: (b, hg, 0, 0)),
        ],
        out_specs=pl.BlockSpec((1, rows, tqs), lambda b, hg, qi: (b, hg, qi)),
        out_shape=jax.ShapeDtypeStruct((bsz, n_heads * d, s), BF16),
        scratch_shapes=[
            pltpu.VMEM((n_sub, g_heads, d + XROWS, tq), BF16),
            pltpu.VMEM((g_heads, tk, tq), F32),
            pltpu.VMEM((g_heads, tk, tq), F32),
            pltpu.VMEM((g_heads, 1, tq), F32),
            pltpu.VMEM((g_heads, 1, tq), F32),
            pltpu.VMEM((n_sub, g_heads, 1, tq), F32),
            pltpu.VMEM((n_sub, g_heads, d + XROWS, tq), F32),
        ],
        compiler_params=pltpu.CompilerParams(
            dimension_semantics=("parallel", "parallel", "arbitrary"),
            vmem_limit_bytes=_vmem_limit(blocks, scratch, 4 * g_heads * tq * tk * 4),
        ),
        name="fox_attention",
    )(pt, pt, pt, pt, qx, kx)


def _swa_attn_kernel(q_ref, k_ref, v_ref, gate_ref, sink_ref, o_ref, *, blk, nsub):
    n = pl.program_id(2)
    d = HEAD_DIM
    g = q_ref.shape[1] // d
    assert blk == SWA_WINDOW
    ones_rows = (lax.broadcasted_iota(jnp.int32, (XROWS, 2 * blk), 0) == 0).astype(BF16)
    sink = sink_ref[0]
    j_loc = lax.broadcasted_iota(jnp.int32, (blk, g * blk), 0)
    t_loc = lax.broadcasted_iota(jnp.int32, (blk, g * blk), 1) & (blk - 1)
    from_prev = j_loc > t_loc

    def window_start(i):
        return pl.multiple_of(jnp.maximum(n * nsub + i - 1, 0) * blk, blk)

    scores = []
    for i in range(nsub):
        qa = jnp.concatenate([q_ref[0, h * d:(h + 1) * d, i * blk:(i + 1) * blk] for h in range(g)], axis=1)
        scores.append(lax.dot_general(k_ref[0, :, pl.ds(window_start(i), 2 * blk)], qa,
                                      (((0,), (0,)), ((), ())), preferred_element_type=F32))

    for i in range(nsub):
        start = window_start(i)
        s_lo, s_hi = scores[i][0:blk], scores[i][blk:2 * blk]
        if i == 0:
            s_lo, s_hi = jnp.where(n == 0, NEG_INF, s_lo), jnp.where(n == 0, s_lo, s_hi)
        s = jnp.where(from_prev, s_lo, s_hi)
        m = jnp.maximum(jnp.max(s, axis=0, keepdims=True), sink)
        e = jnp.exp2(s - m).astype(BF16)
        zero = jnp.zeros_like(e)
        e_win = jnp.concatenate([jnp.where(from_prev, e, zero), jnp.where(from_prev, zero, e)], axis=0)
        if i == 0:
            e_first = jnp.concatenate([jnp.where(from_prev, zero, e), zero], axis=0)
            e_win = jnp.where(n == 0, e_first, e_win)
        va = jnp.concatenate([v_ref[0, :, pl.ds(start, 2 * blk)], ones_rows], axis=0)
        o = jnp.dot(va, e_win, preferred_element_type=F32)
        denom = o[d:d + 1] + jnp.exp2(sink - m)
        y = o[0:d] * (1.0 / denom)
        for h in range(g):
            gate = gate_ref[0, h * d:(h + 1) * d, i * blk:(i + 1) * blk]
            o_ref[0, h * d:(h + 1) * d, i * blk:(i + 1) * blk] = _gated(y[:, h * blk:(h + 1) * blk], gate)


def _swa_attention(pt, sink_rows, *, blk, nsub):
    bsz, _, s = pt.shape
    d = HEAD_DIM
    rows = SWA_GROUP * d
    wq, wk = SWA_Q_HEADS * d, SWA_KV_HEADS * d
    tq = nsub * blk
    k0, v0, g0 = wq // d, (wq + wk) // d, (wq + 2 * wk) // rows
    blocks = 3 * rows * tq * 2 + 2 * d * s * 2 + 8 * SWA_GROUP * blk * 4
    return pl.pallas_call(
        functools.partial(_swa_attn_kernel, blk=blk, nsub=nsub),
        grid=(bsz, SWA_KV_HEADS, s // tq),
        in_specs=[
            pl.BlockSpec((1, rows, tq), lambda b, h, n: (b, h, n)),
            pl.BlockSpec((1, d, s), lambda b, h, n: (b, k0 + h, 0)),
            pl.BlockSpec((1, d, s), lambda b, h, n: (b, v0 + h, 0)),
            pl.BlockSpec((1, rows, tq), lambda b, h, n: (b, g0 + h, n)),
            pl.BlockSpec((1, 1, SWA_GROUP * blk), lambda b, h, n: (h, 0, 0)),
        ],
        out_specs=pl.BlockSpec((1, rows, tq), lambda b, h, n: (b, h, n)),
        out_shape=jax.ShapeDtypeStruct((bsz, wq, s), BF16),
        compiler_params=pltpu.CompilerParams(
            dimension_semantics=("parallel", "parallel", "arbitrary"),
            vmem_limit_bytes=_vmem_limit(blocks, 0, 8 * nsub * 2 * blk * SWA_GROUP * blk * 4),
        ),
        name="swa_attention",
    )(pt, pt, pt, pt, sink_rows)


def _out_proj_kernel(zt_ref, x_ref, w_ref, fg_ref, o_ref, *, final_norm):
    out = x_ref[...] + lax.dot_general(zt_ref[0], w_ref[...], (((0,), (0,)), ((), ())),
                                       preferred_element_type=F32)
    if final_norm:
        ms = jnp.mean(out * out, axis=-1, keepdims=True)
        out = (out * lax.rsqrt(ms + RMS_EPS)) * fg_ref[...]
    o_ref[...] = out


def _out_proj(zt, x2d, w, fg, *, tm, final_norm):
    bsz, wid, s = zt.shape
    m, d = x2d.shape
    sb = s // tm
    blocks = wid * tm * 2 + 2 * tm * d * 4 + wid * d * 2 + d * 4
    return pl.pallas_call(
        functools.partial(_out_proj_kernel, final_norm=final_norm),
        grid=(m // tm,),
        in_specs=[
            pl.BlockSpec((1, wid, tm), lambda i: (i // sb, 0, i % sb)),
            pl.BlockSpec((tm, d), lambda i: (i, 0)),
            pl.BlockSpec((wid, d), lambda i: (0, 0), pipeline_mode=pl.Buffered(1)),
            pl.BlockSpec((1, d), lambda i: (0, 0)),
        ],
        out_specs=pl.BlockSpec((tm, d), lambda i: (i, 0)),
        out_shape=jax.ShapeDtypeStruct((m, d), F32),
        compiler_params=pltpu.CompilerParams(
            dimension_semantics=("parallel",),
            vmem_limit_bytes=_vmem_limit(blocks, 0, wid * tm * 2 + 3 * tm * d * 4),
        ),
        name="out_proj",
    )(zt, x2d, w, fg)


def _rope_tables(s):
    inv_freq = ROPE_THETA ** (-jnp.arange(ROT_HALF, dtype=F32) / ROT_HALF)
    ang = jnp.arange(s, dtype=F32)[:, None] * inv_freq[None, :]
    return jnp.cos(ang).T, jnp.sin(ang).T


PROJ_TM = 1024
FOX_PROJ_TN = 2048
SWA_PROJ_TN = 1536
WT_TN = 256
OUT_TM = 1024
FOX_G, FOX_TQ, FOX_TK = 8, 512, 256
FOX_NSUB = 4
SWA_NSUB = 16


def kernel(x, norm_g, fox_w_in, fox_b_f, fox_w_out, swa_w_in, swa_sinks, swa_w_out, final_g):
    bsz, s, d = x.shape
    x2d = x.reshape(bsz * s, d)

    wid = FOX_HEADS * HEAD_DIM
    wt = fox_w_in[0].T.astype(BF16)
    pt, zt = _norm_proj(x2d, norm_g[0][None, :], wt, bsz=bsz, mode="fox", n_forget=FOX_HEADS,
                        tm=PROJ_TM, tn=FOX_PROJ_TN, q_blocks=wid // FOX_PROJ_TN)
    qx, kx = _decay_rows(zt, fox_b_f[0][:, None], FOX_HEADS)
    yt = _fox_attention(pt, qx, kx, n_heads=FOX_HEADS, g_heads=FOX_G, tq=FOX_TQ, tk=FOX_TK, n_sub=FOX_NSUB)
    x2d = _out_proj(yt, x2d, fox_w_out[0].astype(BF16), final_g[None, :], tm=OUT_TM, final_norm=False)

    wq = SWA_Q_HEADS * HEAD_DIM
    n_swa = swa_w_in.shape[2]
    wt = _transpose_cast(swa_w_in[0], n_swa, WT_TN)
    head = jnp.arange(n_swa // HEAD_DIM)
    head_scale = jnp.where(head < SWA_Q_HEADS, QK_SCALE, 1.0).astype(F32)
    head_rope = (head < SWA_Q_HEADS + SWA_KV_HEADS).astype(F32)
    (pt,) = _norm_proj(x2d, norm_g[1][None, :], wt, bsz=bsz, mode="swa", tm=PROJ_TM, tn=SWA_PROJ_TN,
                       tables=(head_scale, head_rope) + _rope_tables(s))
    sink_rows = jnp.repeat(swa_sinks[0].reshape(SWA_KV_HEADS, SWA_GROUP) * LOG2E, SWA_WINDOW, axis=1)[:, None, :]
    yt = _swa_attention(pt, sink_rows, blk=SWA_WINDOW, nsub=SWA_NSUB)
    out = _out_proj(yt, x2d, swa_w_out[0].astype(BF16), final_g[None, :], tm=OUT_TM, final_norm=True)
    return out.reshape(bsz, s, d)
```

```python
import functools

import jax
import jax.numpy as jnp
from jax import lax
from jax.experimental import pallas as pl
from jax.experimental.pallas import tpu as pltpu

F32 = jnp.float32
BF16 = jnp.bfloat16

RMS_EPS = 1e-6
NEG_INF = -1e30
LOG2E = 1.4426950408889634
HEAD_DIM = 64
FOX_HEADS = 32
SWA_Q_HEADS = 32
SWA_KV_HEADS = 4
SWA_GROUP = SWA_Q_HEADS // SWA_KV_HEADS
SWA_WINDOW = 128
ROPE_THETA = 500000.0
ROT_DIM = HEAD_DIM // 4
ROT_HALF = ROT_DIM // 2
XROWS = 16
VMEM_CAPACITY = 64 * 1024 * 1024
QK_SCALE = HEAD_DIM ** -0.5 * LOG2E


def _gated(y, gate):
    half = 0.5 * gate.astype(F32)
    return (y * (half * (1.0 + jnp.tanh(half)))).astype(gate.dtype)


def _vmem_limit(pipelined_block_bytes, scratch_bytes, temp_bytes):
    need = 2 * pipelined_block_bytes + scratch_bytes + temp_bytes
    return int(min(need, VMEM_CAPACITY - (4 << 20)))


_NT = (((1,), (1,)), ((), ()))


def _norm_to_scratch(x_ref, g_ref, h_ref):
    x = x_ref[...]
    ms = jnp.mean(x * x, axis=-1, keepdims=True)
    h_ref[...] = ((x * lax.rsqrt(ms + RMS_EPS)) * g_ref[...]).astype(BF16)


def _norm_proj_fox_kernel(x_ref, g_ref, wt_ref, wft_ref, o_ref, of_ref, h_ref, *, q_blocks):
    j = pl.program_id(1)

    @pl.when(j == 0)
    def _():
        _norm_to_scratch(x_ref, g_ref, h_ref)
        of_ref[0] = lax.dot_general(wft_ref[...], h_ref[...], _NT, preferred_element_type=F32)

    res = lax.dot_general(wt_ref[...], h_ref[...], _NT, preferred_element_type=F32)
    o_ref[0] = (res * jnp.where(j < q_blocks, QK_SCALE, 1.0)).astype(o_ref.dtype)


def _norm_proj_swa_kernel(scale_ref, rope_ref, x_ref, g_ref, wt_ref, cos_ref, sin_ref, o_ref, h_ref):
    j = pl.program_id(1)
    tn = o_ref.shape[1]
    heads = tn // HEAD_DIM

    @pl.when(j == 0)
    def _():
        _norm_to_scratch(x_ref, g_ref, h_ref)

    res = lax.dot_general(wt_ref[...], h_ref[...], _NT, preferred_element_type=F32)
    cos, sin = cos_ref[...], sin_ref[...]
    for hh in range(heads):
        base = hh * HEAD_DIM
        scale = scale_ref[j * heads + hh]
        rope = rope_ref[j * heads + hh]
        ch = (1.0 + rope * (cos - 1.0)) * scale
        sh = (rope * sin) * scale
        x1 = res[base:base + ROT_HALF]
        x2 = res[base + ROT_HALF:base + ROT_DIM]
        o_ref[0, base:base + ROT_DIM, :] = jnp.concatenate(
            [x1 * ch - x2 * sh, x2 * ch + x1 * sh], axis=0).astype(o_ref.dtype)
        o_ref[0, base + ROT_DIM:base + HEAD_DIM, :] = (res[base + ROT_DIM:base + HEAD_DIM] * scale).astype(o_ref.dtype)


def _transpose_cast_kernel(w_ref, o_ref):
    o_ref[...] = w_ref[...].T.astype(o_ref.dtype)


def _transpose_cast(w, n, tn):
    d = w.shape[0]
    return pl.pallas_call(
        _transpose_cast_kernel,
        grid=(n // tn,),
        in_specs=[pl.BlockSpec((d, tn), lambda j: (0, j))],
        out_specs=pl.BlockSpec((tn, d), lambda j: (j, 0)),
        out_shape=jax.ShapeDtypeStruct((n, d), BF16),
        compiler_params=pltpu.CompilerParams(
            dimension_semantics=("parallel",),
            vmem_limit_bytes=_vmem_limit(d * tn * 4 + tn * d * 2, 0, 2 * d * tn * 4),
        ),
        name="transpose_cast",
    )(w)


def _norm_proj(x2d, g, wt, *, bsz, mode, tm, tn, q_blocks=None, n_forget=0, tables=None):
    m, d = x2d.shape
    n = wt.shape[0] - n_forget
    s = m // bsz
    sb = s // tm
    in_specs = [
        pl.BlockSpec((tm, d), lambda i, j: (i, 0)),
        pl.BlockSpec((1, d), lambda i, j: (0, 0)),
        pl.BlockSpec((tn, d), lambda i, j: (j, 0)),
    ]
    args = [x2d, g, wt]
    out_specs = [pl.BlockSpec((1, tn, tm), lambda i, j: (i // sb, j, i % sb))]
    out_shape = [jax.ShapeDtypeStruct((bsz, n, s), BF16)]
    blocks = tm * d * 4 + d * 4 + tn * d * 2 + tn * tm * 2
    if mode == "fox":
        body = functools.partial(_norm_proj_fox_kernel, q_blocks=q_blocks)
        in_specs.append(pl.BlockSpec((n_forget, d), lambda i, j: (n // n_forget, 0)))
        args.append(wt)
        out_specs.append(pl.BlockSpec((1, n_forget, tm), lambda i, j: (i // sb, 0, i % sb)))
        out_shape.append(jax.ShapeDtypeStruct((bsz, n_forget, s), F32))
        blocks += n_forget * d * 2 + n_forget * tm * 4
    else:
        body = _norm_proj_swa_kernel
        scale, rope, cos, sin = tables
        in_specs = [pl.BlockSpec(memory_space=pltpu.SMEM)] * 2 + in_specs
        in_specs += [pl.BlockSpec((ROT_HALF, tm), lambda i, j: (0, i % sb))] * 2
        args = [scale, rope] + args + [cos, sin]
        blocks += 2 * ROT_HALF * tm * 4
    return pl.pallas_call(
        body,
        grid=(m // tm, n // tn),
        in_specs=in_specs,
        out_specs=out_specs,
        out_shape=out_shape,
        scratch_shapes=[pltpu.VMEM((tm, d), BF16)],
        compiler_params=pltpu.CompilerParams(
            dimension_semantics=("parallel", "arbitrary"),
            vmem_limit_bytes=_vmem_limit(blocks, tm * d * 2, 3 * tm * d * 4 + 2 * tn * tm * 4),
        ),
        name="norm_proj_" + mode,
    )(*args)


def _decay_rows_kernel(z_ref, b_ref, qx_ref, kx_ref):
    z = z_ref[0] + b_ref[...]
    x = (jnp.minimum(z, 0.0) - jnp.log1p(jnp.exp(-jnp.abs(z)))) * LOG2E
    n_heads, s = x.shape
    col = lax.broadcasted_iota(jnp.int32, x.shape, 1)
    shift = 1
    while shift < s:
        x = x + jnp.where(col >= shift, pltpu.roll(x, shift, axis=1), 0.0)
        shift *= 2
    hi = x.astype(BF16).astype(F32)
    mid = (x - hi).astype(BF16).astype(F32)
    lo = (x - hi - mid).astype(BF16).astype(F32)
    row = lax.broadcasted_iota(jnp.int32, (XROWS, s), 0)
    for h in range(n_heads):
        parts_q = jnp.where(row == 0, hi[h:h + 1], jnp.where(row == 1, mid[h:h + 1], lo[h:h + 1]))
        parts_k = jnp.where(row == 3, hi[h:h + 1], jnp.where(row == 4, mid[h:h + 1], lo[h:h + 1]))
        qx = jnp.where(row < 3, parts_q, jnp.where(row < 6, 1.0, 0.0))
        kx = jnp.where(row < 3, 1.0, jnp.where(row < 6, -parts_k, 0.0))
        qx_ref[0, h] = qx.astype(BF16)
        kx_ref[0, h] = kx.astype(BF16)


def _decay_rows(z, b, n_heads):
    bsz, _, s = z.shape
    out_blk = n_heads * XROWS * s * 2
    return pl.pallas_call(
        _decay_rows_kernel,
        grid=(bsz,),
        in_specs=[
            pl.BlockSpec((1, n_heads, s), lambda i: (i, 0, 0)),
            pl.BlockSpec((n_heads, 1), lambda i: (0, 0)),
        ],
        out_specs=[pl.BlockSpec((1, n_heads, XROWS, s), lambda i: (i, 0, 0, 0))] * 2,
        out_shape=[jax.ShapeDtypeStruct((bsz, n_heads, XROWS, s), BF16)] * 2,
        compiler_params=pltpu.CompilerParams(
            dimension_semantics=("parallel",),
            vmem_limit_bytes=_vmem_limit(n_heads * s * 4 + 2 * out_blk, 0, 8 * n_heads * s * 4),
        ),
        name="decay_rows",
    )(z, b)


def _fox_attn_kernel(q_ref, k_ref, v_ref, gate_ref, qx_ref, kx_ref, o_ref,
                     qa_ref, s0_ref, s1_ref, bm0_ref, bm1_ref, m_ref, acc_ref, *, g_heads, tq, tk, n_sub):
    tiles = []
    for sub in range(n_sub):
        cols = slice(sub * tq, (sub + 1) * tq)
        tiles.append(_fox_query_tile(pl.program_id(2) * n_sub + sub,
                                     q_ref.at[:, :, cols], k_ref, v_ref, gate_ref.at[:, :, cols],
                                     qx_ref.at[:, :, :, cols], kx_ref, o_ref.at[:, :, cols],
                                     qa_ref.at[sub], s0_ref, s1_ref, bm0_ref, bm1_ref, m_ref.at[sub],
                                     acc_ref.at[sub], g_heads=g_heads, tq=tq, tk=tk))
    for init, _, _ in tiles:
        init()
    for g in range(g_heads):
        tiles[0][1](g)
    for sub, (_, _, run) in enumerate(tiles):
        run(tiles[sub + 1][1] if sub + 1 < n_sub else None)


def _fox_query_tile(qi, q_ref, k_ref, v_ref, gate_ref, qx_ref, kx_ref, o_ref,
                    qa_ref, s0_ref, s1_ref, bm0_ref, bm1_ref, m_ref, acc_ref, *, g_heads, tq, tk):
    n_diag = tq // tk
    n_full = qi * n_diag
    d = HEAD_DIM
    s_refs = (s0_ref, s1_ref)
    bm_refs = (bm0_ref, bm1_ref)

    def init():
        for g in range(g_heads):
            qa_ref[g, 0:d, :] = q_ref[0, g * d:(g + 1) * d, :]
            qa_ref[g, d:d + XROWS, :] = qx_ref[0, g]
        m_ref[...] = jnp.full(m_ref.shape, NEG_INF, F32)
        acc_ref[...] = jnp.zeros(acc_ref.shape, F32)

    ones_rows = (lax.broadcasted_iota(jnp.int32, (XROWS, tk), 0) == 0).astype(BF16)

    def scores(j, slot, g, col0=0):
        off = pl.multiple_of(j * tk, tk)
        ka = jnp.concatenate([k_ref[0, g * d:(g + 1) * d, pl.ds(off, tk)],
                              kx_ref[0, g, :, pl.ds(off, tk)]], axis=0)
        s = lax.dot_general(ka, qa_ref[g, :, col0:], (((0,), (0,)), ((), ())),
                            preferred_element_type=F32)
        s_refs[slot][g, :, col0:] = s
        bm_refs[slot][g, :, col0:] = jnp.max(s, axis=0, keepdims=True)

    def softmax_pv(j, slot, g, diag):
        col0 = 0 if diag is None else diag * tk
        off = pl.multiple_of(j * tk, tk)
        s = s_refs[slot][g, :, col0:]
        if diag is None:
            bm = bm_refs[slot][g]
        else:
            krow = lax.broadcasted_iota(jnp.int32, s.shape, 0)
            qcol = lax.broadcasted_iota(jnp.int32, s.shape, 1)
            s = jnp.where(krow <= qcol, s, NEG_INF)
            bm = jnp.max(s, axis=0, keepdims=True)
        m_prev = m_ref[g, :, col0:]
        m_new = jnp.maximum(m_prev, bm)
        alpha = jnp.exp2(m_prev - m_new)
        p = jnp.exp2(s - m_new).astype(BF16)
        va = jnp.concatenate([v_ref[0, g * d:(g + 1) * d, pl.ds(off, tk)], ones_rows], axis=0)
        acc_ref[g, :, col0:] = alpha * acc_ref[g, :, col0:] + jnp.dot(va, p, preferred_element_type=F32)
        m_ref[g, :, col0:] = m_new

    def step(j, slot, diag, next_col0=0):
        for g in range(g_heads):
            if next_col0 is not None:
                scores(j + 1, 1 - slot, g, next_col0)
            softmax_pv(j, slot, g, diag)

    def first_scores(g):
        scores(0, 0, g)

    def steps(j0, count):
        for u in range(count):
            step(j0 + u, u % 2, None)

    def run(next_first_scores):
        assert n_diag == 2
        rem = n_full % 4

        @pl.when(rem == 2)
        def _():
            steps(0, 2)

        def four_steps(jj, carry):
            steps(rem + 4 * jj, 4)
            return carry

        lax.fori_loop(0, n_full // 4, four_steps, 0)
        step(n_full, 0, 0, next_col0=tk)
        for g in range(g_heads):
            if next_first_scores is not None:
                next_first_scores(g)
            softmax_pv(n_full + 1, 1, g, 1)

        for g in range(g_heads):
            y = acc_ref[g, 0:d, :] * (1.0 / acc_ref[g, d:d + 1, :])
            o_ref[0, g * d:(g + 1) * d, :] = _gated(y, gate_ref[0, g * d:(g + 1) * d, :])

    return init, first_scores, run


def _fox_attention(pt, qx, kx, *, n_heads, g_heads, tq, tk, n_sub):
    bsz, _, s = pt.shape
    d = HEAD_DIM
    rows = g_heads * d
    nw = n_heads // g_heads
    tqs = n_sub * tq
    blocks = 3 * rows * tqs * 2 + 2 * rows * s * 2 + g_heads * XROWS * (tqs + s) * 2
    scratch = g_heads * (n_sub * ((d + XROWS) * tq * 2 + 8 * tq * 4 + (d + XROWS) * tq * 4)
                         + 2 * 8 * tq * 4 + 2 * tk * tq * 4)
    return pl.pallas_call(
        functools.partial(_fox_attn_kernel, g_heads=g_heads, tq=tq, tk=tk, n_sub=n_sub),
        grid=(bsz, nw, s // tqs),
        in_specs=[
            pl.BlockSpec((1, rows, tqs), lambda b, hg, qi: (b, hg, qi)),
            pl.BlockSpec((1, rows, s), lambda b, hg, qi: (b, nw + hg, 0)),
            pl.BlockSpec((1, rows, s), lambda b, hg, qi: (b, 2 * nw + hg, 0)),
            pl.BlockSpec((1, rows, tqs), lambda b, hg, qi: (b, 3 * nw + hg, qi)),
            pl.BlockSpec((1, g_heads, XROWS, tqs), lambda b, hg, qi: (b, hg, 0, qi)),
            pl.BlockSpec((1, g_heads, XROWS, s), lambda b, hg, qi: (b, hg, 0, 0)),
        ],
        out_specs=pl.BlockSpec((1, rows, tqs), lambda b, hg, qi: (b, hg, qi)),
        out_shape=jax.ShapeDtypeStruct((bsz, n_heads * d, s), BF16),
        scratch_shapes=[
            pltpu.VMEM((n_sub, g_heads, d + XROWS, tq), BF16),
            pltpu.VMEM((g_heads, tk, tq), F32),
            pltpu.VMEM((g_heads, tk, tq), F32),
            pltpu.VMEM((g_heads, 1, tq), F32),
            pltpu.VMEM((g_heads, 1, tq), F32),
            pltpu.VMEM((n_sub, g_heads, 1, tq), F32),
            pltpu.VMEM((n_sub, g_heads, d + XROWS, tq), F32),
        ],
        compiler_params=pltpu.CompilerParams(
            dimension_semantics=("parallel", "parallel", "arbitrary"),
            vmem_limit_bytes=_vmem_limit(blocks, scratch, 4 * g_heads * tq * tk * 4),
        ),
        name="fox_attention",
    )(pt, pt, pt, pt, qx, kx)


def _swa_attn_kernel(q_ref, k_ref, v_ref, gate_ref, sink_ref, o_ref, *, blk, nsub):
    n = pl.program_id(2)
    d = HEAD_DIM
    g = q_ref.shape[1] // d
    assert blk == SWA_WINDOW
    ones_rows = (lax.broadcasted_iota(jnp.int32, (XROWS, 2 * blk), 0) == 0).astype(BF16)
    sink = sink_ref[0]
    j_loc = lax.broadcasted_iota(jnp.int32, (blk, g * blk), 0)
    t_loc = lax.broadcasted_iota(jnp.int32, (blk, g * blk), 1) & (blk - 1)
    from_prev = j_loc > t_loc

    def window_start(i):
        return pl.multiple_of(jnp.maximum(n * nsub + i - 1, 0) * blk, blk)

    scores = []
    for i in range(nsub):
        qa = jnp.concatenate([q_ref[0, h * d:(h + 1) * d, i * blk:(i + 1) * blk] for h in range(g)], axis=1)
        scores.append(lax.dot_general(k_ref[0, :, pl.ds(window_start(i), 2 * blk)], qa,
                                      (((0,), (0,)), ((), ())), preferred_element_type=F32))

    for i in range(nsub):
        start = window_start(i)
        s_lo, s_hi = scores[i][0:blk], scores[i][blk:2 * blk]
        if i == 0:
            s_lo, s_hi = jnp.where(n == 0, NEG_INF, s_lo), jnp.where(n == 0, s_lo, s_hi)
        s = jnp.where(from_prev, s_lo, s_hi)
        m = jnp.maximum(jnp.max(s, axis=0, keepdims=True), sink)
        e = jnp.exp2(s - m).astype(BF16)
        zero = jnp.zeros_like(e)
        e_win = jnp.concatenate([jnp.where(from_prev, e, zero), jnp.where(from_prev, zero, e)], axis=0)
        if i == 0:
            e_first = jnp.concatenate([jnp.where(from_prev, zero, e), zero], axis=0)
            e_win = jnp.where(n == 0, e_first, e_win)
        va = jnp.concatenate([v_ref[0, :, pl.ds(start, 2 * blk)], ones_rows], axis=0)
        o = jnp.dot(va, e_win, preferred_element_type=F32)
        denom = o[d:d + 1] + jnp.exp2(sink - m)
        y = o[0:d] * (1.0 / denom)
        for h in range(g):
            gate = gate_ref[0, h * d:(h + 1) * d, i * blk:(i + 1) * blk]
            o_ref[0, h * d:(h + 1) * d, i * blk:(i + 1) * blk] = _gated(y[:, h * blk:(h + 1) * blk], gate)


def _swa_attention(pt, sink_rows, *, blk, nsub):
    bsz, _, s = pt.shape
    d = HEAD_DIM
    rows = SWA_GROUP * d
    wq, wk = SWA_Q_HEADS * d, SWA_KV_HEADS * d
    tq = nsub * blk
    k0, v0, g0 = wq // d, (wq + wk) // d, (wq + 2 * wk) // rows
    blocks = 3 * rows * tq * 2 + 2 * d * s * 2 + 8 * SWA_GROUP * blk * 4
    return pl.pallas_call(
        functools.partial(_swa_attn_kernel, blk=blk, nsub=nsub),
        grid=(bsz, SWA_KV_HEADS, s // tq),
        in_specs=[
            pl.BlockSpec((1, rows, tq), lambda b, h, n: (b, h, n)),
            pl.BlockSpec((1, d, s), lambda b, h, n: (b, k0 + h, 0)),
            pl.BlockSpec((1, d, s), lambda b, h, n: (b, v0 + h, 0)),
            pl.BlockSpec((1, rows, tq), lambda b, h, n: (b, g0 + h, n)),
            pl.BlockSpec((1, 1, SWA_GROUP * blk), lambda b, h, n: (h, 0, 0)),
        ],
        out_specs=pl.BlockSpec((1, rows, tq), lambda b, h, n: (b, h, n)),
        out_shape=jax.ShapeDtypeStruct((bsz, wq, s), BF16),
        compiler_params=pltpu.CompilerParams(
            dimension_semantics=("parallel", "parallel", "arbitrary"),
            vmem_limit_bytes=_vmem_limit(blocks, 0, 8 * nsub * 2 * blk * SWA_GROUP * blk * 4),
        ),
        name="swa_attention",
    )(pt, pt, pt, pt, sink_rows)


def _out_proj_kernel(zt_ref, x_ref, w_ref, fg_ref, o_ref, *, final_norm):
    out = x_ref[...] + lax.dot_general(zt_ref[0], w_ref[...], (((0,), (0,)), ((), ())),
                                       preferred_element_type=F32)
    if final_norm:
        ms = jnp.mean(out * out, axis=-1, keepdims=True)
        out = (out * lax.rsqrt(ms + RMS_EPS)) * fg_ref[...]
    o_ref[...] = out


def _out_proj(zt, x2d, w, fg, *, tm, final_norm):
    bsz, wid, s = zt.shape
    m, d = x2d.shape
    sb = s // tm
    blocks = wid * tm * 2 + 2 * tm * d * 4 + wid * d * 2 + d * 4
    return pl.pallas_call(
        functools.partial(_out_proj_kernel, final_norm=final_norm),
        grid=(m // tm,),
        in_specs=[
            pl.BlockSpec((1, wid, tm), lambda i: (i // sb, 0, i % sb)),
            pl.BlockSpec((tm, d), lambda i: (i, 0)),
            pl.BlockSpec((wid, d), lambda i: (0, 0), pipeline_mode=pl.Buffered(1)),
            pl.BlockSpec((1, d), lambda i: (0, 0)),
        ],
        out_specs=pl.BlockSpec((tm, d), lambda i: (i, 0)),
        out_shape=jax.ShapeDtypeStruct((m, d), F32),
        compiler_params=pltpu.CompilerParams(
            dimension_semantics=("parallel",),
            vmem_limit_bytes=_vmem_limit(blocks, 0, wid * tm * 2 + 3 * tm * d * 4),
        ),
        name="out_proj",
    )(zt, x2d, w, fg)


def _rope_tables(s):
    inv_freq = ROPE_THETA ** (-jnp.arange(ROT_HALF, dtype=F32) / ROT_HALF)
    ang = jnp.arange(s, dtype=F32)[:, None] * inv_freq[None, :]
    return jnp.cos(ang).T, jnp.sin(ang).T


PROJ_TM = 1024
FOX_PROJ_TN = 2048
SWA_PROJ_TN = 1536
WT_TN = 512
OUT_TM = 1024
FOX_G, FOX_TQ, FOX_TK = 8, 512, 256
FOX_NSUB = 4
SWA_NSUB = 32


def kernel(x, norm_g, fox_w_in, fox_b_f, fox_w_out, swa_w_in, swa_sinks, swa_w_out, final_g):
    bsz, s, d = x.shape
    x2d = x.reshape(bsz * s, d)

    wid = FOX_HEADS * HEAD_DIM
    wt = fox_w_in[0].T.astype(BF16)
    pt, zt = _norm_proj(x2d, norm_g[0][None, :], wt, bsz=bsz, mode="fox", n_forget=FOX_HEADS,
                        tm=PROJ_TM, tn=FOX_PROJ_TN, q_blocks=wid // FOX_PROJ_TN)
    qx, kx = _decay_rows(zt, fox_b_f[0][:, None], FOX_HEADS)
    yt = _fox_attention(pt, qx, kx, n_heads=FOX_HEADS, g_heads=FOX_G, tq=FOX_TQ, tk=FOX_TK, n_sub=FOX_NSUB)
    x2d = _out_proj(yt, x2d, fox_w_out[0].astype(BF16), final_g[None, :], tm=OUT_TM, final_norm=False)

    wq = SWA_Q_HEADS * HEAD_DIM
    n_swa = swa_w_in.shape[2]
    wt = _transpose_cast(swa_w_in[0], n_swa, WT_TN)
    head = jnp.arange(n_swa // HEAD_DIM)
    head_scale = jnp.where(head < SWA_Q_HEADS, QK_SCALE, 1.0).astype(F32)
    head_rope = (head < SWA_Q_HEADS + SWA_KV_HEADS).astype(F32)
    (pt,) = _norm_proj(x2d, norm_g[1][None, :], wt, bsz=bsz, mode="swa", tm=PROJ_TM, tn=SWA_PROJ_TN,
                       tables=(head_scale, head_rope) + _rope_tables(s))
    sink_rows = jnp.repeat(swa_sinks[0].reshape(SWA_KV_HEADS, SWA_GROUP) * LOG2E, SWA_WINDOW, axis=1)[:, None, :]
    yt = _swa_attention(pt, sink_rows, blk=SWA_WINDOW, nsub=SWA_NSUB)
    out = _out_proj(yt, x2d, swa_w_out[0].astype(BF16), final_g[None, :], tm=OUT_TM, final_norm=True)
    return out.reshape(bsz, s, d)
```

```python
import functools

import jax
import jax.numpy as jnp
from jax import lax
from jax.experimental import pallas as pl
from jax.experimental.pallas import tpu as pltpu

F32 = jnp.float32
BF16 = jnp.bfloat16

RMS_EPS = 1e-6
NEG_INF = -1e30
LOG2E = 1.4426950408889634
HEAD_DIM = 64
FOX_HEADS = 32
SWA_Q_HEADS = 32
SWA_KV_HEADS = 4
SWA_GROUP = SWA_Q_HEADS // SWA_KV_HEADS
SWA_WINDOW = 128
ROPE_THETA = 500000.0
ROT_DIM = HEAD_DIM // 4
ROT_HALF = ROT_DIM // 2
XROWS = 16
VMEM_CAPACITY = 64 * 1024 * 1024
QK_SCALE = HEAD_DIM ** -0.5 * LOG2E


def _gated(y, gate):
    half = 0.5 * gate.astype(F32)
    return (y * (half * (1.0 + jnp.tanh(half)))).astype(gate.dtype)


def _vmem_limit(pipelined_block_bytes, scratch_bytes, temp_bytes):
    need = 2 * pipelined_block_bytes + scratch_bytes + temp_bytes
    return int(min(need, VMEM_CAPACITY - (4 << 20)))


_NT = (((1,), (1,)), ((), ()))


def _norm_to_scratch(x_ref, g_ref, h_ref):
    x = x_ref[...]
    ms = jnp.mean(x * x, axis=-1, keepdims=True)
    h_ref[...] = ((x * lax.rsqrt(ms + RMS_EPS)) * g_ref[...]).astype(BF16)


def _norm_proj_fox_kernel(x_ref, g_ref, wt_ref, wft_ref, o_ref, of_ref, h_ref, *, q_blocks):
    j = pl.program_id(1)

    @pl.when(j == 0)
    def _():
        _norm_to_scratch(x_ref, g_ref, h_ref)
        of_ref[0] = lax.dot_general(wft_ref[...], h_ref[...], _NT, preferred_element_type=F32)

    res = lax.dot_general(wt_ref[...], h_ref[...], _NT, preferred_element_type=F32)
    o_ref[0] = (res * jnp.where(j < q_blocks, QK_SCALE, 1.0)).astype(o_ref.dtype)


def _norm_proj_swa_kernel(scale_ref, rope_ref, x_ref, g_ref, wt_ref, cos_ref, sin_ref, o_ref, h_ref):
    j = pl.program_id(1)
    tn = o_ref.shape[1]
    heads = tn // HEAD_DIM

    @pl.when(j == 0)
    def _():
        _norm_to_scratch(x_ref, g_ref, h_ref)

    res = lax.dot_general(wt_ref[...], h_ref[...], _NT, preferred_element_type=F32)
    cos, sin = cos_ref[...], sin_ref[...]
    for hh in range(heads):
        base = hh * HEAD_DIM
        scale = scale_ref[j * heads + hh]
        rope = rope_ref[j * heads + hh]
        ch = (1.0 + rope * (cos - 1.0)) * scale
        sh = (rope * sin) * scale
        x1 = res[base:base + ROT_HALF]
        x2 = res[base + ROT_HALF:base + ROT_DIM]
        o_ref[0, base:base + ROT_DIM, :] = jnp.concatenate(
            [x1 * ch - x2 * sh, x2 * ch + x1 * sh], axis=0).astype(o_ref.dtype)
        o_ref[0, base + ROT_DIM:base + HEAD_DIM, :] = (res[base + ROT_DIM:base + HEAD_DIM] * scale).astype(o_ref.dtype)


def _transpose_cast_kernel(w_ref, o_ref):
    o_ref[...] = w_ref[...].T.astype(o_ref.dtype)


def _transpose_cast(w, n, tn):
    d = w.shape[0]
    return pl.pallas_call(
        _transpose_cast_kernel,
        grid=(n // tn,),
        in_specs=[pl.BlockSpec((d, tn), lambda j: (0, j))],
        out_specs=pl.BlockSpec((tn, d), lambda j: (j, 0)),
        out_shape=jax.ShapeDtypeStruct((n, d), BF16),
        compiler_params=pltpu.CompilerParams(
            dimension_semantics=("parallel",),
            vmem_limit_bytes=_vmem_limit(d * tn * 4 + tn * d * 2, 0, 2 * d * tn * 4),
        ),
        name="transpose_cast",
    )(w)


def _norm_proj(x2d, g, wt, *, bsz, mode, tm, tn, q_blocks=None, n_forget=0, tables=None):
    m, d = x2d.shape
    n = wt.shape[0] - n_forget
    s = m // bsz
    sb = s // tm
    in_specs = [
        pl.BlockSpec((tm, d), lambda i, j: (i, 0)),
        pl.BlockSpec((1, d), lambda i, j: (0, 0)),
        pl.BlockSpec((tn, d), lambda i, j: (j, 0)),
    ]
    args = [x2d, g, wt]
    out_specs = [pl.BlockSpec((1, tn, tm), lambda i, j: (i // sb, j, i % sb))]
    out_shape = [jax.ShapeDtypeStruct((bsz, n, s), BF16)]
    blocks = tm * d * 4 + d * 4 + tn * d * 2 + tn * tm * 2
    if mode == "fox":
        body = functools.partial(_norm_proj_fox_kernel, q_blocks=q_blocks)
        in_specs.append(pl.BlockSpec((n_forget, d), lambda i, j: (n // n_forget, 0)))
        args.append(wt)
        out_specs.append(pl.BlockSpec((1, n_forget, tm), lambda i, j: (i // sb, 0, i % sb)))
        out_shape.append(jax.ShapeDtypeStruct((bsz, n_forget, s), F32))
        blocks += n_forget * d * 2 + n_forget * tm * 4
    else:
        body = _norm_proj_swa_kernel
        scale, rope, cos, sin = tables
        in_specs = [pl.BlockSpec(memory_space=pltpu.SMEM)] * 2 + in_specs
        in_specs += [pl.BlockSpec((ROT_HALF, tm), lambda i, j: (0, i % sb))] * 2
        args = [scale, rope] + args + [cos, sin]
        blocks += 2 * ROT_HALF * tm * 4
    return pl.pallas_call(
        body,
        grid=(m // tm, n // tn),
        in_specs=in_specs,
        out_specs=out_specs,
        out_shape=out_shape,
        scratch_shapes=[pltpu.VMEM((tm, d), BF16)],
        compiler_params=pltpu.CompilerParams(
            dimension_semantics=("parallel", "arbitrary"),
            vmem_limit_bytes=_vmem_limit(blocks, tm * d * 2, 3 * tm * d * 4 + 2 * tn * tm * 4),
        ),
        name="norm_proj_" + mode,
    )(*args)


def _decay_rows_kernel(z_ref, b_ref, qx_ref, kx_ref):
    z = z_ref[0] + b_ref[...]
    x = (jnp.minimum(z, 0.0) - jnp.log1p(jnp.exp(-jnp.abs(z)))) * LOG2E
    n_heads, s = x.shape
    col = lax.broadcasted_iota(jnp.int32, x.shape, 1)
    shift = 1
    while shift < s:
        x = x + jnp.where(col >= shift, pltpu.roll(x, shift, axis=1), 0.0)
        shift *= 2
    hi = x.astype(BF16).astype(F32)
    mid = (x - hi).astype(BF16).astype(F32)
    lo = (x - hi - mid).astype(BF16).astype(F32)
    row = lax.broadcasted_iota(jnp.int32, (XROWS, s), 0)
    for h in range(n_heads):
        parts_q = jnp.where(row == 0, hi[h:h + 1], jnp.where(row == 1, mid[h:h + 1], lo[h:h + 1]))
        parts_k = jnp.where(row == 3, hi[h:h + 1], jnp.where(row == 4, mid[h:h + 1], lo[h:h + 1]))
        qx = jnp.where(row < 3, parts_q, jnp.where(row < 6, 1.0, 0.0))
        kx = jnp.where(row < 3, 1.0, jnp.where(row < 6, -parts_k, 0.0))
        qx_ref[0, h] = qx.astype(BF16)
        kx_ref[0, h] = kx.astype(BF16)


def _decay_rows(z, b, n_heads):
    bsz, _, s = z.shape
    out_blk = n_heads * XROWS * s * 2
    return pl.pallas_call(
        _decay_rows_kernel,
        grid=(bsz,),
        in_specs=[
            pl.BlockSpec((1, n_heads, s), lambda i: (i, 0, 0)),
            pl.BlockSpec((n_heads, 1), lambda i: (0, 0)),
        ],
        out_specs=[pl.BlockSpec((1, n_heads, XROWS, s), lambda i: (i, 0, 0, 0))] * 2,
        out_shape=[jax.ShapeDtypeStruct((bsz, n_heads, XROWS, s), BF16)] * 2,
        compiler_params=pltpu.CompilerParams(
            dimension_semantics=("parallel",),
            vmem_limit_bytes=_vmem_limit(n_heads * s * 4 + 2 * out_blk, 0, 8 * n_heads * s * 4),
        ),
        name="decay_rows",
    )(z, b)


def _fox_attn_kernel(q_ref, k_ref, v_ref, gate_ref, qx_ref, kx_ref, o_ref,
                     qa_ref, s0_ref, s1_ref, bm0_ref, bm1_ref, m_ref, acc_ref, *, g_heads, tq, tk, n_sub):
    tiles = []
    for sub in range(n_sub):
        cols = slice(sub * tq, (sub + 1) * tq)
        tiles.append(_fox_query_tile(pl.program_id(2) * n_sub + sub,
                                     q_ref.at[:, :, cols], k_ref, v_ref, gate_ref.at[:, :, cols],
                                     qx_ref.at[:, :, :, cols], kx_ref, o_ref.at[:, :, cols],
                                     qa_ref.at[sub], s0_ref, s1_ref, bm0_ref, bm1_ref, m_ref.at[sub],
                                     acc_ref.at[sub], g_heads=g_heads, tq=tq, tk=tk))
    for init, _, _ in tiles:
        init()
    for g in range(g_heads):
        tiles[0][1](g)
    for sub, (_, _, run) in enumerate(tiles):
        run(tiles[sub + 1][1] if sub + 1 < n_sub else None)


def _fox_query_tile(qi, q_ref, k_ref, v_ref, gate_ref, qx_ref, kx_ref, o_ref,
                    qa_ref, s0_ref, s1_ref, bm0_ref, bm1_ref, m_ref, acc_ref, *, g_heads, tq, tk):
    n_diag = tq // tk
    n_full = qi * n_diag
    d = HEAD_DIM
    s_refs = (s0_ref, s1_ref)
    bm_refs = (bm0_ref, bm1_ref)

    def init():
        for g in range(g_heads):
            qa_ref[g, 0:d, :] = q_ref[0, g * d:(g + 1) * d, :]
            qa_ref[g, d:d + XROWS, :] = qx_ref[0, g]
        m_ref[...] = jnp.full(m_ref.shape, NEG_INF, F32)
        acc_ref[...] = jnp.zeros(acc_ref.shape, F32)

    ones_rows = (lax.broadcasted_iota(jnp.int32, (XROWS, tk), 0) == 0).astype(BF16)

    def scores(j, slot, g, col0=0):
        off = pl.multiple_of(j * tk, tk)
        ka = jnp.concatenate([k_ref[0, g * d:(g + 1) * d, pl.ds(off, tk)],
                              kx_ref[0, g, :, pl.ds(off, tk)]], axis=0)
        s = lax.dot_general(ka, qa_ref[g, :, col0:], (((0,), (0,)), ((), ())),
                            preferred_element_type=F32)
        s_refs[slot][g, :, col0:] = s
        if col0 == 0:
            bm_refs[slot][g] = jnp.max(s, axis=0, keepdims=True)

    def softmax_pv(j, slot, g, diag):
        col0 = 0 if diag is None else diag * tk
        off = pl.multiple_of(j * tk, tk)
        s = s_refs[slot][g, :, col0:]
        if diag is None:
            bm = bm_refs[slot][g]
        else:
            krow = lax.broadcasted_iota(jnp.int32, (tk, tk), 0)
            qcol = lax.broadcasted_iota(jnp.int32, (tk, tk), 1)
            tri = jnp.where(krow <= qcol, s[:, :tk], NEG_INF)
            bm = jnp.max(tri, axis=0, keepdims=True)
            if s.shape[1] > tk:
                s = jnp.concatenate([tri, s[:, tk:]], axis=1)
                bm = jnp.concatenate([bm, bm_refs[slot][g, :, col0 + tk:]], axis=1)
            else:
                s = tri
        m_prev = m_ref[g, :, col0:]
        m_new = jnp.maximum(m_prev, bm)
        alpha = jnp.exp2(m_prev - m_new)
        p = jnp.exp2(s - m_new).astype(BF16)
        va = jnp.concatenate([v_ref[0, g * d:(g + 1) * d, pl.ds(off, tk)], ones_rows], axis=0)
        acc_ref[g, :, col0:] = alpha * acc_ref[g, :, col0:] + jnp.dot(va, p, preferred_element_type=F32)
        m_ref[g, :, col0:] = m_new

    def step(j, slot, diag, next_col0=0):
        for g in range(g_heads):
            if next_col0 is not None:
                scores(j + 1, 1 - slot, g, next_col0)
            softmax_pv(j, slot, g, diag)

    def first_scores(g):
        scores(0, 0, g)

    def steps(j0, count):
        for u in range(count):
            step(j0 + u, u % 2, None)

    def run(next_first_scores):
        assert n_diag == 2
        rem = n_full % 4

        @pl.when(rem == 2)
        def _():
            steps(0, 2)

        def four_steps(jj, carry):
            steps(rem + 4 * jj, 4)
            return carry

        lax.fori_loop(0, n_full // 4, four_steps, 0)
        step(n_full, 0, 0, next_col0=tk)
        for g in range(g_heads):
            if next_first_scores is not None:
                next_first_scores(g)
            softmax_pv(n_full + 1, 1, g, 1)

        for g in range(g_heads):
            y = acc_ref[g, 0:d, :] * (1.0 / acc_ref[g, d:d + 1, :])
            o_ref[0, g * d:(g + 1) * d, :] = _gated(y, gate_ref[0, g * d:(g + 1) * d, :])

    return init, first_scores, run


def _fox_attention(pt, qx, kx, *, n_heads, g_heads, tq, tk, n_sub):
    bsz, _, s = pt.shape
    d = HEAD_DIM
    rows = g_heads * d
    nw = n_heads // g_heads
    tqs = n_sub * tq
    blocks = 3 * rows * tqs * 2 + 2 * rows * s * 2 + g_heads * XROWS * (tqs + s) * 2
    scratch = g_heads * (n_sub * ((d + XROWS) * tq * 2 + 8 * tq * 4 + (d + XROWS) * tq * 4)
                         + 2 * 8 * tq * 4 + 2 * tk * tq * 4)
    return pl.pallas_call(
        functools.partial(_fox_attn_kernel, g_heads=g_heads, tq=tq, tk=tk, n_sub=n_sub),
        grid=(bsz, nw, s // tqs),
        in_specs=[
            pl.BlockSpec((1, rows, tqs), lambda b, hg, qi: (b, hg, qi)),
            pl.BlockSpec((1, rows, s), lambda b, hg, qi: (b, nw + hg, 0)),
            pl.BlockSpec((1, rows, s), lambda b, hg, qi: (b, 2 * nw + hg, 0)),
            pl.BlockSpec((1, rows, tqs), lambda b, hg, qi: (b, 3 * nw + hg, qi)),
            pl.BlockSpec((1, g_heads, XROWS, tqs), lambda b, hg, qi: (b, hg, 0, qi)),
            pl.BlockSpec((1, g_heads, XROWS, s), lambda b, hg, qi: (b, hg, 0, 0)),
        ],
        out_specs=pl.BlockSpec((1, rows, tqs), lambda b, hg, qi: (b, hg, qi)),
        out_shape=jax.ShapeDtypeStruct((bsz, n_heads * d, s), BF16),
        scratch_shapes=[
            pltpu.VMEM((n_sub, g_heads, d + XROWS, tq), BF16),
            pltpu.VMEM((g_heads, tk, tq), F32),
            pltpu.VMEM((g_heads, tk, tq), F32),
            pltpu.VMEM((g_heads, 1, tq), F32),
            pltpu.VMEM((g_heads, 1, tq), F32),
            pltpu.VMEM((n_sub, g_heads, 1, tq), F32),
            pltpu.VMEM((n_sub, g_heads, d + XROWS, tq), F32),
        ],
        compiler_params=pltpu.CompilerParams(
            dimension_semantics=("parallel", "parallel", "arbitrary"),
            vmem_limit_bytes=_vmem_limit(blocks, scratch, 4 * g_heads * tq * tk * 4),
        ),
        name="fox_attention",
    )(pt, pt, pt, pt, qx, kx)


def _swa_attn_kernel(q_ref, k_ref, v_ref, gate_ref, sink_ref, o_ref, *, blk, nsub):
    n = pl.program_id(2)
    d = HEAD_DIM
    g = q_ref.shape[1] // d
    assert blk == SWA_WINDOW
    ones_rows = (lax.broadcasted_iota(jnp.int32, (XROWS, 2 * blk), 0) == 0).astype(BF16)
    sink = sink_ref[0]
    j_loc = lax.broadcasted_iota(jnp.int32, (blk, g * blk), 0)
    t_loc = lax.broadcasted_iota(jnp.int32, (blk, g * blk), 1) & (blk - 1)
    from_prev = j_loc > t_loc

    def window_start(i):
        return pl.multiple_of(jnp.maximum(n * nsub + i - 1, 0) * blk, blk)

    scores = []
    for i in range(nsub):
        qa = jnp.concatenate([q_ref[0, h * d:(h + 1) * d, i * blk:(i + 1) * blk] for h in range(g)], axis=1)
        scores.append(lax.dot_general(k_ref[0, :, pl.ds(window_start(i), 2 * blk)], qa,
                                      (((0,), (0,)), ((), ())), preferred_element_type=F32))

    for i in range(nsub):
        start = window_start(i)
        s_lo, s_hi = scores[i][0:blk], scores[i][blk:2 * blk]
        if i == 0:
            s_lo, s_hi = jnp.where(n == 0, NEG_INF, s_lo), jnp.where(n == 0, s_lo, s_hi)
        s = jnp.where(from_prev, s_lo, s_hi)
        m = jnp.maximum(jnp.max(s, axis=0, keepdims=True), sink)
        e = jnp.exp2(s - m).astype(BF16)
        zero = jnp.zeros_like(e)
        e_win = jnp.concatenate([jnp.where(from_prev, e, zero), jnp.where(from_prev, zero, e)], axis=0)
        if i == 0:
            e_first = jnp.concatenate([jnp.where(from_prev, zero, e), zero], axis=0)
            e_win = jnp.where(n == 0, e_first, e_win)
        va = jnp.concatenate([v_ref[0, :, pl.ds(start, 2 * blk)], ones_rows], axis=0)
        o = jnp.dot(va, e_win, preferred_element_type=F32)
        denom = o[d:d + 1] + jnp.exp2(sink - m)
        y = o[0:d] * (1.0 / denom)
        for h in range(g):
            gate = gate_ref[0, h * d:(h + 1) * d, i * blk:(i + 1) * blk]
            o_ref[0, h * d:(h + 1) * d, i * blk:(i + 1) * blk] = _gated(y[:, h * blk:(h + 1) * blk], gate)


def _swa_attention(pt, sink_rows, *, blk, nsub):
    bsz, _, s = pt.shape
    d = HEAD_DIM
    rows = SWA_GROUP * d
    wq, wk = SWA_Q_HEADS * d, SWA_KV_HEADS * d
    tq = nsub * blk
    k0, v0, g0 = wq // d, (wq + wk) // d, (wq + 2 * wk) // rows
    blocks = 3 * rows * tq * 2 + 2 * d * s * 2 + 8 * SWA_GROUP * blk * 4
    return pl.pallas_call(
        functools.partial(_swa_attn_kernel, blk=blk, nsub=nsub),
        grid=(bsz, SWA_KV_HEADS, s // tq),
        in_specs=[
            pl.BlockSpec((1, rows, tq), lambda b, h, n: (b, h, n)),
            pl.BlockSpec((1, d, s), lambda b, h, n: (b, k0 + h, 0)),
            pl.BlockSpec((1, d, s), lambda b, h, n: (b, v0 + h, 0)),
            pl.BlockSpec((1, rows, tq), lambda b, h, n: (b, g0 + h, n)),
            pl.BlockSpec((1, 1, SWA_GROUP * blk), lambda b, h, n: (h, 0, 0)),
        ],
        out_specs=pl.BlockSpec((1, rows, tq), lambda b, h, n: (b, h, n)),
        out_shape=jax.ShapeDtypeStruct((bsz, wq, s), BF16),
        compiler_params=pltpu.CompilerParams(
            dimension_semantics=("parallel", "parallel", "arbitrary"),
            vmem_limit_bytes=_vmem_limit(blocks, 0, 8 * nsub * 2 * blk * SWA_GROUP * blk * 4),
        ),
        name="swa_attention",
    )(pt, pt, pt, pt, sink_rows)


def _out_proj_kernel(zt_ref, x_ref, w_ref, fg_ref, o_ref, *, final_norm):
    out = x_ref[...] + lax.dot_general(zt_ref[0], w_ref[...], (((0,), (0,)), ((), ())),
                                       preferred_element_type=F32)
    if final_norm:
        ms = jnp.mean(out * out, axis=-1, keepdims=True)
        out = (out * lax.rsqrt(ms + RMS_EPS)) * fg_ref[...]
    o_ref[...] = out


def _out_proj(zt, x2d, w, fg, *, tm, final_norm):
    bsz, wid, s = zt.shape
    m, d = x2d.shape
    sb = s // tm
    blocks = wid * tm * 2 + 2 * tm * d * 4 + wid * d * 2 + d * 4
    return pl.pallas_call(
        functools.partial(_out_proj_kernel, final_norm=final_norm),
        grid=(m // tm,),
        in_specs=[
            pl.BlockSpec((1, wid, tm), lambda i: (i // sb, 0, i % sb)),
            pl.BlockSpec((tm, d), lambda i: (i, 0)),
            pl.BlockSpec((wid, d), lambda i: (0, 0), pipeline_mode=pl.Buffered(1)),
            pl.BlockSpec((1, d), lambda i: (0, 0)),
        ],
        out_specs=pl.BlockSpec((tm, d), lambda i: (i, 0)),
        out_shape=jax.ShapeDtypeStruct((m, d), F32),
        compiler_params=pltpu.CompilerParams(
            dimension_semantics=("parallel",),
            vmem_limit_bytes=_vmem_limit(blocks, 0, wid * tm * 2 + 3 * tm * d * 4),
        ),
        name="out_proj",
    )(zt, x2d, w, fg)


def _rope_tables(s):
    inv_freq = ROPE_THETA ** (-jnp.arange(ROT_HALF, dtype=F32) / ROT_HALF)
    ang = jnp.arange(s, dtype=F32)[:, None] * inv_freq[None, :]
    return jnp.cos(ang).T, jnp.sin(ang).T


PROJ_TM = 1024
FOX_PROJ_TN = 2048
SWA_PROJ_TN = 1536
WT_TN = 512
OUT_TM = 1024
FOX_G, FOX_TQ, FOX_TK = 8, 512, 256
FOX_NSUB = 4
SWA_NSUB = 32


def kernel(x, norm_g, fox_w_in, fox_b_f, fox_w_out, swa_w_in, swa_sinks, swa_w_out, final_g):
    bsz, s, d = x.shape
    x2d = x.reshape(bsz * s, d)

    wid = FOX_HEADS * HEAD_DIM
    wt = fox_w_in[0].T.astype(BF16)
    pt, zt = _norm_proj(x2d, norm_g[0][None, :], wt, bsz=bsz, mode="fox", n_forget=FOX_HEADS,
                        tm=PROJ_TM, tn=FOX_PROJ_TN, q_blocks=wid // FOX_PROJ_TN)
    qx, kx = _decay_rows(zt, fox_b_f[0][:, None], FOX_HEADS)
    yt = _fox_attention(pt, qx, kx, n_heads=FOX_HEADS, g_heads=FOX_G, tq=FOX_TQ, tk=FOX_TK, n_sub=FOX_NSUB)
    x2d = _out_proj(yt, x2d, fox_w_out[0].astype(BF16), final_g[None, :], tm=OUT_TM, final_norm=False)

    wq = SWA_Q_HEADS * HEAD_DIM
    n_swa = swa_w_in.shape[2]
    wt = _transpose_cast(swa_w_in[0], n_swa, WT_TN)
    head = jnp.arange(n_swa // HEAD_DIM)
    head_scale = jnp.where(head < SWA_Q_HEADS, QK_SCALE, 1.0).astype(F32)
    head_rope = (head < SWA_Q_HEADS + SWA_KV_HEADS).astype(F32)
    (pt,) = _norm_proj(x2d, norm_g[1][None, :], wt, bsz=bsz, mode="swa", tm=PROJ_TM, tn=SWA_PROJ_TN,
                       tables=(head_scale, head_rope) + _rope_tables(s))
    sink_rows = jnp.repeat(swa_sinks[0].reshape(SWA_KV_HEADS, SWA_GROUP) * LOG2E, SWA_WINDOW, axis=1)[:, None, :]
    yt = _swa_attention(pt, sink_rows, blk=SWA_WINDOW, nsub=SWA_NSUB)
    out = _out_proj(yt, x2d, swa_w_out[0].astype(BF16), final_g[None, :], tm=OUT_TM, final_norm=True)
    return out.reshape(bsz, s, d)
```

```python
import functools

import jax
import jax.numpy as jnp
from jax import lax
from jax.experimental import pallas as pl
from jax.experimental.pallas import tpu as pltpu

F32 = jnp.float32
BF16 = jnp.bfloat16

RMS_EPS = 1e-6
NEG_INF = -1e30
LOG2E = 1.4426950408889634
HEAD_DIM = 64
FOX_HEADS = 32
SWA_Q_HEADS = 32
SWA_KV_HEADS = 4
SWA_GROUP = SWA_Q_HEADS // SWA_KV_HEADS
SWA_WINDOW = 128
ROPE_THETA = 500000.0
ROT_DIM = HEAD_DIM // 4
ROT_HALF = ROT_DIM // 2
XROWS = 16
VMEM_CAPACITY = 64 * 1024 * 1024
QK_SCALE = HEAD_DIM ** -0.5 * LOG2E


def _gated(y, gate):
    half = 0.5 * gate.astype(F32)
    return (y * (half * (1.0 + jnp.tanh(half)))).astype(gate.dtype)


def _vmem_limit(pipelined_block_bytes, scratch_bytes, temp_bytes):
    need = 2 * pipelined_block_bytes + scratch_bytes + temp_bytes
    return int(min(need, VMEM_CAPACITY - (4 << 20)))


_NT = (((1,), (1,)), ((), ()))


def _norm_to_scratch(x_ref, g_ref, h_ref):
    x = x_ref[...]
    ms = jnp.mean(x * x, axis=-1, keepdims=True)
    h_ref[...] = ((x * lax.rsqrt(ms + RMS_EPS)) * g_ref[...]).astype(BF16)


def _norm_proj_fox_kernel(x_ref, g_ref, wt_ref, wft_ref, o_ref, of_ref, h_ref, *, q_blocks):
    j = pl.program_id(1)

    @pl.when(j == 0)
    def _():
        _norm_to_scratch(x_ref, g_ref, h_ref)
        of_ref[0] = lax.dot_general(wft_ref[...], h_ref[...], _NT, preferred_element_type=F32)

    res = lax.dot_general(wt_ref[...], h_ref[...], _NT, preferred_element_type=F32)
    o_ref[0] = (res * jnp.where(j < q_blocks, QK_SCALE, 1.0)).astype(o_ref.dtype)


def _norm_proj_swa_kernel(scale_ref, rope_ref, x_ref, g_ref, wt_ref, cos_ref, sin_ref, o_ref, h_ref):
    j = pl.program_id(1)
    tn = o_ref.shape[1]
    heads = tn // HEAD_DIM

    @pl.when(j == 0)
    def _():
        _norm_to_scratch(x_ref, g_ref, h_ref)

    res = lax.dot_general(wt_ref[...], h_ref[...], _NT, preferred_element_type=F32)
    cos, sin = cos_ref[...], sin_ref[...]
    for hh in range(heads):
        base = hh * HEAD_DIM
        scale = scale_ref[j * heads + hh]
        rope = rope_ref[j * heads + hh]
        ch = (1.0 + rope * (cos - 1.0)) * scale
        sh = (rope * sin) * scale
        x1 = res[base:base + ROT_HALF]
        x2 = res[base + ROT_HALF:base + ROT_DIM]
        o_ref[0, base:base + ROT_DIM, :] = jnp.concatenate(
            [x1 * ch - x2 * sh, x2 * ch + x1 * sh], axis=0).astype(o_ref.dtype)
        o_ref[0, base + ROT_DIM:base + HEAD_DIM, :] = (res[base + ROT_DIM:base + HEAD_DIM] * scale).astype(o_ref.dtype)


def _transpose_cast_kernel(w_ref, o_ref):
    o_ref[...] = w_ref[...].T.astype(o_ref.dtype)


def _transpose_cast(w, n, tn):
    d = w.shape[0]
    return pl.pallas_call(
        _transpose_cast_kernel,
        grid=(n // tn,),
        in_specs=[pl.BlockSpec((d, tn), lambda j: (0, j))],
        out_specs=pl.BlockSpec((tn, d), lambda j: (j, 0)),
        out_shape=jax.ShapeDtypeStruct((n, d), BF16),
        compiler_params=pltpu.CompilerParams(
            dimension_semantics=("parallel",),
            vmem_limit_bytes=_vmem_limit(d * tn * 4 + tn * d * 2, 0, 2 * d * tn * 4),
        ),
        name="transpose_cast",
    )(w)


def _norm_proj(x2d, g, wt, *, bsz, mode, tm, tn, q_blocks=None, n_forget=0, tables=None):
    m, d = x2d.shape
    n = wt.shape[0] - n_forget
    s = m // bsz
    sb = s // tm
    in_specs = [
        pl.BlockSpec((tm, d), lambda i, j: (i, 0)),
        pl.BlockSpec((1, d), lambda i, j: (0, 0)),
        pl.BlockSpec((tn, d), lambda i, j: (j, 0)),
    ]
    args = [x2d, g, wt]
    out_specs = [pl.BlockSpec((1, tn, tm), lambda i, j: (i // sb, j, i % sb))]
    out_shape = [jax.ShapeDtypeStruct((bsz, n, s), BF16)]
    blocks = tm * d * 4 + d * 4 + tn * d * 2 + tn * tm * 2
    if mode == "fox":
        body = functools.partial(_norm_proj_fox_kernel, q_blocks=q_blocks)
        in_specs.append(pl.BlockSpec((n_forget, d), lambda i, j: (n // n_forget, 0)))
        args.append(wt)
        out_specs.append(pl.BlockSpec((1, n_forget, tm), lambda i, j: (i // sb, 0, i % sb)))
        out_shape.append(jax.ShapeDtypeStruct((bsz, n_forget, s), F32))
        blocks += n_forget * d * 2 + n_forget * tm * 4
    else:
        body = _norm_proj_swa_kernel
        scale, rope, cos, sin = tables
        in_specs = [pl.BlockSpec(memory_space=pltpu.SMEM)] * 2 + in_specs
        in_specs += [pl.BlockSpec((ROT_HALF, tm), lambda i, j: (0, i % sb))] * 2
        args = [scale, rope] + args + [cos, sin]
        blocks += 2 * ROT_HALF * tm * 4
    return pl.pallas_call(
        body,
        grid=(m // tm, n // tn),
        in_specs=in_specs,
        out_specs=out_specs,
        out_shape=out_shape,
        scratch_shapes=[pltpu.VMEM((tm, d), BF16)],
        compiler_params=pltpu.CompilerParams(
            dimension_semantics=("parallel", "arbitrary"),
            vmem_limit_bytes=_vmem_limit(blocks, tm * d * 2, 3 * tm * d * 4 + 2 * tn * tm * 4),
        ),
        name="norm_proj_" + mode,
    )(*args)


def _decay_rows_kernel(z_ref, b_ref, qx_ref, kx_ref):
    z = z_ref[0] + b_ref[...]
    x = (jnp.minimum(z, 0.0) - jnp.log1p(jnp.exp(-jnp.abs(z)))) * LOG2E
    n_heads, s = x.shape
    col = lax.broadcasted_iota(jnp.int32, x.shape, 1)
    shift = 1
    while shift < s:
        x = x + jnp.where(col >= shift, pltpu.roll(x, shift, axis=1), 0.0)
        shift *= 2
    hi = x.astype(BF16).astype(F32)
    mid = (x - hi).astype(BF16).astype(F32)
    lo = (x - hi - mid).astype(BF16).astype(F32)
    row = lax.broadcasted_iota(jnp.int32, (XROWS, s), 0)
    for h in range(n_heads):
        parts_q = jnp.where(row == 0, hi[h:h + 1], jnp.where(row == 1, mid[h:h + 1], lo[h:h + 1]))
        parts_k = jnp.where(row == 3, hi[h:h + 1], jnp.where(row == 4, mid[h:h + 1], lo[h:h + 1]))
        qx = jnp.where(row < 3, parts_q, jnp.where(row < 6, 1.0, 0.0))
        kx = jnp.where(row < 3, 1.0, jnp.where(row < 6, -parts_k, 0.0))
        qx_ref[0, h] = qx.astype(BF16)
        kx_ref[0, h] = kx.astype(BF16)


def _decay_rows(z, b, n_heads):
    bsz, _, s = z.shape
    out_blk = n_heads * XROWS * s * 2
    return pl.pallas_call(
        _decay_rows_kernel,
        grid=(bsz,),
        in_specs=[
            pl.BlockSpec((1, n_heads, s), lambda i: (i, 0, 0)),
            pl.BlockSpec((n_heads, 1), lambda i: (0, 0)),
        ],
        out_specs=[pl.BlockSpec((1, n_heads, XROWS, s), lambda i: (i, 0, 0, 0))] * 2,
        out_shape=[jax.ShapeDtypeStruct((bsz, n_heads, XROWS, s), BF16)] * 2,
        compiler_params=pltpu.CompilerParams(
            dimension_semantics=("parallel",),
            vmem_limit_bytes=_vmem_limit(n_heads * s * 4 + 2 * out_blk, 0, 8 * n_heads * s * 4),
        ),
        name="decay_rows",
    )(z, b)


def _fox_attn_kernel(q_ref, k_ref, v_ref, gate_ref, qx_ref, kx_ref, o_ref,
                     qa_ref, s0_ref, s1_ref, bm0_ref, bm1_ref, m_ref, acc_ref, *, g_heads, tq, tk, n_sub):
    tiles = []
    for sub in range(n_sub):
        cols = slice(sub * tq, (sub + 1) * tq)
        tiles.append(_fox_query_tile(pl.program_id(2) * n_sub + sub,
                                     q_ref.at[:, :, cols], k_ref, v_ref, gate_ref.at[:, :, cols],
                                     qx_ref.at[:, :, :, cols], kx_ref, o_ref.at[:, :, cols],
                                     qa_ref.at[sub], s0_ref, s1_ref, bm0_ref, bm1_ref, m_ref.at[sub],
                                     acc_ref.at[sub], g_heads=g_heads, tq=tq, tk=tk))
    for init, _, _ in tiles:
        init()
    for g in range(g_heads):
        tiles[0][1](g)
    for sub, (_, _, run) in enumerate(tiles):
        run(tiles[sub + 1][1] if sub + 1 < n_sub else None)


def _fox_query_tile(qi, q_ref, k_ref, v_ref, gate_ref, qx_ref, kx_ref, o_ref,
                    qa_ref, s0_ref, s1_ref, bm0_ref, bm1_ref, m_ref, acc_ref, *, g_heads, tq, tk):
    n_diag = tq // tk
    n_full = qi * n_diag
    d = HEAD_DIM
    s_refs = (s0_ref, s1_ref)
    bm_refs = (bm0_ref, bm1_ref)

    def init():
        for g in range(g_heads):
            qa_ref[g, 0:d, :] = q_ref[0, g * d:(g + 1) * d, :]
            qa_ref[g, d:d + XROWS, :] = qx_ref[0, g]
        m_ref[...] = jnp.full(m_ref.shape, NEG_INF, F32)
        acc_ref[...] = jnp.zeros(acc_ref.shape, F32)

    ones_rows = (lax.broadcasted_iota(jnp.int32, (XROWS, tk), 0) == 0).astype(BF16)

    def scores(j, slot, g, col0=0):
        off = pl.multiple_of(j * tk, tk)
        ka = jnp.concatenate([k_ref[0, g * d:(g + 1) * d, pl.ds(off, tk)],
                              kx_ref[0, g, :, pl.ds(off, tk)]], axis=0)
        s = lax.dot_general(ka, qa_ref[g, :, col0:], (((0,), (0,)), ((), ())),
                            preferred_element_type=F32)
        s_refs[slot][g, :, col0:] = s
        if col0 == 0:
            bm_refs[slot][g] = jnp.max(s, axis=0, keepdims=True)

    def softmax_pv(j, slot, g, diag):
        col0 = 0 if diag is None else diag * tk
        off = pl.multiple_of(j * tk, tk)
        s = s_refs[slot][g, :, col0:]
        m_prev = m_ref[g, :, col0:]
        if diag is None:
            m_new = jnp.maximum(m_prev, bm_refs[slot][g])
            p = jnp.exp2(s - m_new).astype(BF16)
        else:
            hb = tk // 2
            keep = (lax.broadcasted_iota(jnp.int32, (hb, hb), 0) <= lax.broadcasted_iota(jnp.int32, (hb, hb), 1))
            tl = jnp.where(keep, s[0:hb, 0:hb], NEG_INF)
            tr = s[0:hb, hb:tk]
            br = jnp.where(keep, s[hb:tk, hb:tk], NEG_INF)
            bm = [jnp.max(tl, axis=0, keepdims=True),
                  jnp.maximum(jnp.max(tr, axis=0, keepdims=True), jnp.max(br, axis=0, keepdims=True))]
            if s.shape[1] > tk:
                bm.append(bm_refs[slot][g, :, col0 + tk:])
            m_new = jnp.maximum(m_prev, jnp.concatenate(bm, axis=1))
            m_l, m_r = m_new[:, 0:hb], m_new[:, hb:tk]
            top = jnp.concatenate([jnp.exp2(tl - m_l), jnp.exp2(tr - m_r)], axis=1).astype(BF16)
            bottom = jnp.concatenate([jnp.zeros((hb, hb), BF16), jnp.exp2(br - m_r).astype(BF16)], axis=1)
            p = jnp.concatenate([top, bottom], axis=0)
            if s.shape[1] > tk:
                p = jnp.concatenate([p, jnp.exp2(s[:, tk:] - m_new[:, tk:]).astype(BF16)], axis=1)
        alpha = jnp.exp2(m_prev - m_new)
        va = jnp.concatenate([v_ref[0, g * d:(g + 1) * d, pl.ds(off, tk)], ones_rows], axis=0)
        acc_ref[g, :, col0:] = alpha * acc_ref[g, :, col0:] + jnp.dot(va, p, preferred_element_type=F32)
        m_ref[g, :, col0:] = m_new

    def step(j, slot, diag, next_col0=0):
        for g in range(g_heads):
            if next_col0 is not None:
                scores(j + 1, 1 - slot, g, next_col0)
            softmax_pv(j, slot, g, diag)

    def first_scores(g):
        scores(0, 0, g)

    def steps(j0, count):
        for u in range(count):
            step(j0 + u, u % 2, None)

    def run(next_first_scores):
        assert n_diag == 2
        rem = n_full % 4

        @pl.when(rem == 2)
        def _():
            steps(0, 2)

        def four_steps(jj, carry):
            steps(rem + 4 * jj, 4)
            return carry

        lax.fori_loop(0, n_full // 4, four_steps, 0)
        step(n_full, 0, 0, next_col0=tk)
        for g in range(g_heads):
            if next_first_scores is not None:
                next_first_scores(g)
            softmax_pv(n_full + 1, 1, g, 1)

        for g in range(g_heads):
            y = acc_ref[g, 0:d, :] * (1.0 / acc_ref[g, d:d + 1, :])
            o_ref[0, g * d:(g + 1) * d, :] = _gated(y, gate_ref[0, g * d:(g + 1) * d, :])

    return init, first_scores, run


def _fox_attention(pt, qx, kx, *, n_heads, g_heads, tq, tk, n_sub):
    bsz, _, s = pt.shape
    d = HEAD_DIM
    rows = g_heads * d
    nw = n_heads // g_heads
    tqs = n_sub * tq
    blocks = 3 * rows * tqs * 2 + 2 * rows * s * 2 + g_heads * XROWS * (tqs + s) * 2
    scratch = g_heads * (n_sub * ((d + XROWS) * tq * 2 + 8 * tq * 4 + (d + XROWS) * tq * 4)
                         + 2 * 8 * tq * 4 + 2 * tk * tq * 4)
    return pl.pallas_call(
        functools.partial(_fox_attn_kernel, g_heads=g_heads, tq=tq, tk=tk, n_sub=n_sub),
        grid=(bsz, nw, s // tqs),
        in_specs=[
            pl.BlockSpec((1, rows, tqs), lambda b, hg, qi: (b, hg, qi)),
            pl.BlockSpec((1, rows, s), lambda b, hg, qi: (b, nw + hg, 0)),
            pl.BlockSpec((1, rows, s), lambda b, hg, qi: (b, 2 * nw + hg, 0)),
            pl.BlockSpec((1, rows, tqs), lambda b, hg, qi: (b, 3 * nw + hg, qi)),
            pl.BlockSpec((1, g_heads, XROWS, tqs), lambda b, hg, qi: (b, hg, 0, qi)),
            pl.BlockSpec((1, g_heads, XROWS, s), lambda b, hg, qi: (b, hg, 0, 0)),
        ],
        out_specs=pl.BlockSpec((1, rows, tqs), lambda b, hg, qi: (b, hg, qi)),
        out_shape=jax.ShapeDtypeStruct((bsz, n_heads * d, s), BF16),
        scratch_shapes=[
            pltpu.VMEM((n_sub, g_heads, d + XROWS, tq), BF16),
            pltpu.VMEM((g_heads, tk, tq), F32),
            pltpu.VMEM((g_heads, tk, tq), F32),
            pltpu.VMEM((g_heads, 1, tq), F32),
            pltpu.VMEM((g_heads, 1, tq), F32),
            pltpu.VMEM((n_sub, g_heads, 1, tq), F32),
            pltpu.VMEM((n_sub, g_heads, d + XROWS, tq), F32),
        ],
        compiler_params=pltpu.CompilerParams(
            dimension_semantics=("parallel", "parallel", "arbitrary"),
            vmem_limit_bytes=_vmem_limit(blocks, scratch, 4 * g_heads * tq * tk * 4),
        ),
        name="fox_attention",
    )(pt, pt, pt, pt, qx, kx)


def _swa_attn_kernel(q_ref, k_ref, v_ref, gate_ref, sink_ref, o_ref, *, blk, nsub):
    n = pl.program_id(2)
    d = HEAD_DIM
    g = q_ref.shape[1] // d
    assert blk == SWA_WINDOW
    ones_rows = (lax.broadcasted_iota(jnp.int32, (XROWS, 2 * blk), 0) == 0).astype(BF16)
    sink = sink_ref[0]
    j_loc = lax.broadcasted_iota(jnp.int32, (blk, g * blk), 0)
    t_loc = lax.broadcasted_iota(jnp.int32, (blk, g * blk), 1) & (blk - 1)
    from_prev = j_loc > t_loc

    def window_start(i):
        return pl.multiple_of(jnp.maximum(n * nsub + i - 1, 0) * blk, blk)

    scores = []
    for i in range(nsub):
        qa = jnp.concatenate([q_ref[0, h * d:(h + 1) * d, i * blk:(i + 1) * blk] for h in range(g)], axis=1)
        scores.append(lax.dot_general(k_ref[0, :, pl.ds(window_start(i), 2 * blk)], qa,
                                      (((0,), (0,)), ((), ())), preferred_element_type=F32))

    for i in range(nsub):
        start = window_start(i)
        s_lo, s_hi = scores[i][0:blk], scores[i][blk:2 * blk]
        if i == 0:
            s_lo, s_hi = jnp.where(n == 0, NEG_INF, s_lo), jnp.where(n == 0, s_lo, s_hi)
        s = jnp.where(from_prev, s_lo, s_hi)
        m = jnp.maximum(jnp.max(s, axis=0, keepdims=True), sink)
        e = jnp.exp2(s - m).astype(BF16)
        zero = jnp.zeros_like(e)
        e_win = jnp.concatenate([jnp.where(from_prev, e, zero), jnp.where(from_prev, zero, e)], axis=0)
        if i == 0:
            e_first = jnp.concatenate([jnp.where(from_prev, zero, e), zero], axis=0)
            e_win = jnp.where(n == 0, e_first, e_win)
        va = jnp.concatenate([v_ref[0, :, pl.ds(start, 2 * blk)], ones_rows], axis=0)
        o = jnp.dot(va, e_win, preferred_element_type=F32)
        denom = o[d:d + 1] + jnp.exp2(sink - m)
        y = o[0:d] * (1.0 / denom)
        for h in range(g):
            gate = gate_ref[0, h * d:(h + 1) * d, i * blk:(i + 1) * blk]
            o_ref[0, h * d:(h + 1) * d, i * blk:(i + 1) * blk] = _gated(y[:, h * blk:(h + 1) * blk], gate)


def _swa_attention(pt, sink_rows, *, blk, nsub):
    bsz, _, s = pt.shape
    d = HEAD_DIM
    rows = SWA_GROUP * d
    wq, wk = SWA_Q_HEADS * d, SWA_KV_HEADS * d
    tq = nsub * blk
    k0, v0, g0 = wq // d, (wq + wk) // d, (wq + 2 * wk) // rows
    blocks = 3 * rows * tq * 2 + 2 * d * s * 2 + 8 * SWA_GROUP * blk * 4
    return pl.pallas_call(
        functools.partial(_swa_attn_kernel, blk=blk, nsub=nsub),
        grid=(bsz, SWA_KV_HEADS, s // tq),
        in_specs=[
            pl.BlockSpec((1, rows, tq), lambda b, h, n: (b, h, n)),
            pl.BlockSpec((1, d, s), lambda b, h, n: (b, k0 + h, 0)),
            pl.BlockSpec((1, d, s), lambda b, h, n: (b, v0 + h, 0)),
            pl.BlockSpec((1, rows, tq), lambda b, h, n: (b, g0 + h, n)),
            pl.BlockSpec((1, 1, SWA_GROUP * blk), lambda b, h, n: (h, 0, 0)),
        ],
        out_specs=pl.BlockSpec((1, rows, tq), lambda b, h, n: (b, h, n)),
        out_shape=jax.ShapeDtypeStruct((bsz, wq, s), BF16),
        compiler_params=pltpu.CompilerParams(
            dimension_semantics=("parallel", "parallel", "arbitrary"),
            vmem_limit_bytes=_vmem_limit(blocks, 0, 8 * nsub * 2 * blk * SWA_GROUP * blk * 4),
        ),
        name="swa_attention",
    )(pt, pt, pt, pt, sink_rows)


def _out_proj_kernel(zt_ref, x_ref, w_ref, fg_ref, o_ref, *, final_norm):
    out = x_ref[...] + lax.dot_general(zt_ref[0], w_ref[...], (((0,), (0,)), ((), ())),
                                       preferred_element_type=F32)
    if final_norm:
        ms = jnp.mean(out * out, axis=-1, keepdims=True)
        out = (out * lax.rsqrt(ms + RMS_EPS)) * fg_ref[...]
    o_ref[...] = out


def _out_proj(zt, x2d, w, fg, *, tm, final_norm):
    bsz, wid, s = zt.shape
    m, d = x2d.shape
    sb = s // tm
    blocks = wid * tm * 2 + 2 * tm * d * 4 + wid * d * 2 + d * 4
    return pl.pallas_call(
        functools.partial(_out_proj_kernel, final_norm=final_norm),
        grid=(m // tm,),
        in_specs=[
            pl.BlockSpec((1, wid, tm), lambda i: (i // sb, 0, i % sb)),
            pl.BlockSpec((tm, d), lambda i: (i, 0)),
            pl.BlockSpec((wid, d), lambda i: (0, 0), pipeline_mode=pl.Buffered(1)),
            pl.BlockSpec((1, d), lambda i: (0, 0)),
        ],
        out_specs=pl.BlockSpec((tm, d), lambda i: (i, 0)),
        out_shape=jax.ShapeDtypeStruct((m, d), F32),
        compiler_params=pltpu.CompilerParams(
            dimension_semantics=("parallel",),
            vmem_limit_bytes=_vmem_limit(blocks, 0, wid * tm * 2 + 3 * tm * d * 4),
        ),
        name="out_proj",
    )(zt, x2d, w, fg)


def _rope_tables(s):
    inv_freq = ROPE_THETA ** (-jnp.arange(ROT_HALF, dtype=F32) / ROT_HALF)
    ang = jnp.arange(s, dtype=F32)[:, None] * inv_freq[None, :]
    return jnp.cos(ang).T, jnp.sin(ang).T


PROJ_TM = 1024
FOX_PROJ_TN = 2048
SWA_PROJ_TN = 1536
WT_TN = 512
OUT_TM = 1024
FOX_G, FOX_TQ, FOX_TK = 8, 512, 256
FOX_NSUB = 4
SWA_NSUB = 32


def kernel(x, norm_g, fox_w_in, fox_b_f, fox_w_out, swa_w_in, swa_sinks, swa_w_out, final_g):
    bsz, s, d = x.shape
    x2d = x.reshape(bsz * s, d)

    wid = FOX_HEADS * HEAD_DIM
    wt = fox_w_in[0].T.astype(BF16)
    pt, zt = _norm_proj(x2d, norm_g[0][None, :], wt, bsz=bsz, mode="fox", n_forget=FOX_HEADS,
                        tm=PROJ_TM, tn=FOX_PROJ_TN, q_blocks=wid // FOX_PROJ_TN)
    qx, kx = _decay_rows(zt, fox_b_f[0][:, None], FOX_HEADS)
    yt = _fox_attention(pt, qx, kx, n_heads=FOX_HEADS, g_heads=FOX_G, tq=FOX_TQ, tk=FOX_TK, n_sub=FOX_NSUB)
    x2d = _out_proj(yt, x2d, fox_w_out[0].astype(BF16), final_g[None, :], tm=OUT_TM, final_norm=False)

    wq = SWA_Q_HEADS * HEAD_DIM
    n_swa = swa_w_in.shape[2]
    wt = _transpose_cast(swa_w_in[0], n_swa, WT_TN)
    head = jnp.arange(n_swa // HEAD_DIM)
    head_scale = jnp.where(head < SWA_Q_HEADS, QK_SCALE, 1.0).astype(F32)
    head_rope = (head < SWA_Q_HEADS + SWA_KV_HEADS).astype(F32)
    (pt,) = _norm_proj(x2d, norm_g[1][None, :], wt, bsz=bsz, mode="swa", tm=PROJ_TM, tn=SWA_PROJ_TN,
                       tables=(head_scale, head_rope) + _rope_tables(s))
    sink_rows = jnp.repeat(swa_sinks[0].reshape(SWA_KV_HEADS, SWA_GROUP) * LOG2E, SWA_WINDOW, axis=1)[:, None, :]
    yt = _swa_attention(pt, sink_rows, blk=SWA_WINDOW, nsub=SWA_NSUB)
    out = _out_proj(yt, x2d, swa_w_out[0].astype(BF16), final_g[None, :], tm=OUT_TM, final_norm=True)
    return out.reshape(bsz, s, d)
```

```python
import functools

import jax
import jax.numpy as jnp
from jax import lax
from jax.experimental import pallas as pl
from jax.experimental.pallas import tpu as pltpu

F32 = jnp.float32
BF16 = jnp.bfloat16

RMS_EPS = 1e-6
NEG_INF = -1e30
LOG2E = 1.4426950408889634
HEAD_DIM = 64
FOX_HEADS = 32
SWA_Q_HEADS = 32
SWA_KV_HEADS = 4
SWA_GROUP = SWA_Q_HEADS // SWA_KV_HEADS
SWA_WINDOW = 128
ROPE_THETA = 500000.0
ROT_DIM = HEAD_DIM // 4
ROT_HALF = ROT_DIM // 2
XROWS = 16
VMEM_CAPACITY = 64 * 1024 * 1024
QK_SCALE = HEAD_DIM ** -0.5 * LOG2E


def _gated(y, gate):
    half = 0.5 * gate.astype(F32)
    return (y * (half * (1.0 + jnp.tanh(half)))).astype(gate.dtype)


def _vmem_limit(pipelined_block_bytes, scratch_bytes, temp_bytes):
    need = 2 * pipelined_block_bytes + scratch_bytes + temp_bytes
    return int(min(need, VMEM_CAPACITY - (4 << 20)))


_NT = (((1,), (1,)), ((), ()))


def _norm_to_scratch(x_ref, g_ref, h_ref):
    x = x_ref[...]
    ms = jnp.mean(x * x, axis=-1, keepdims=True)
    h_ref[...] = ((x * lax.rsqrt(ms + RMS_EPS)) * g_ref[...]).astype(BF16)


def _norm_proj_fox_kernel(x_ref, g_ref, wt_ref, wft_ref, o_ref, of_ref, h_ref, *, q_blocks):
    j = pl.program_id(1)

    @pl.when(j == 0)
    def _():
        _norm_to_scratch(x_ref, g_ref, h_ref)
        of_ref[0] = lax.dot_general(wft_ref[...], h_ref[...], _NT, preferred_element_type=F32)

    res = lax.dot_general(wt_ref[...], h_ref[...], _NT, preferred_element_type=F32)
    o_ref[0] = (res * jnp.where(j < q_blocks, QK_SCALE, 1.0)).astype(o_ref.dtype)


def _norm_proj_swa_kernel(scale_ref, rope_ref, x_ref, g_ref, wt_ref, cos_ref, sin_ref, o_ref, h_ref):
    j = pl.program_id(1)
    tn = o_ref.shape[1]
    heads = tn // HEAD_DIM

    @pl.when(j == 0)
    def _():
        _norm_to_scratch(x_ref, g_ref, h_ref)

    res = lax.dot_general(wt_ref[...], h_ref[...], _NT, preferred_element_type=F32)
    cos, sin = cos_ref[...], sin_ref[...]
    for hh in range(heads):
        base = hh * HEAD_DIM
        scale = scale_ref[j * heads + hh]
        rope = rope_ref[j * heads + hh]
        ch = (1.0 + rope * (cos - 1.0)) * scale
        sh = (rope * sin) * scale
        x1 = res[base:base + ROT_HALF]
        x2 = res[base + ROT_HALF:base + ROT_DIM]
        o_ref[0, base:base + ROT_DIM, :] = jnp.concatenate(
            [x1 * ch - x2 * sh, x2 * ch + x1 * sh], axis=0).astype(o_ref.dtype)
        o_ref[0, base + ROT_DIM:base + HEAD_DIM, :] = (res[base + ROT_DIM:base + HEAD_DIM] * scale).astype(o_ref.dtype)


def _transpose_cast_kernel(w_ref, o_ref):
    o_ref[...] = w_ref[...].T.astype(o_ref.dtype)


def _transpose_cast(w, n, tn):
    d = w.shape[0]
    return pl.pallas_call(
        _transpose_cast_kernel,
        grid=(n // tn,),
        in_specs=[pl.BlockSpec((d, tn), lambda j: (0, j))],
        out_specs=pl.BlockSpec((tn, d), lambda j: (j, 0)),
        out_shape=jax.ShapeDtypeStruct((n, d), BF16),
        compiler_params=pltpu.CompilerParams(
            dimension_semantics=("parallel",),
            vmem_limit_bytes=_vmem_limit(d * tn * 4 + tn * d * 2, 0, 2 * d * tn * 4),
        ),
        name="transpose_cast",
    )(w)


def _norm_proj(x2d, g, wt, *, bsz, mode, tm, tn, q_blocks=None, n_forget=0, tables=None):
    m, d = x2d.shape
    n = wt.shape[0] - n_forget
    s = m // bsz
    sb = s // tm
    in_specs = [
        pl.BlockSpec((tm, d), lambda i, j: (i, 0)),
        pl.BlockSpec((1, d), lambda i, j: (0, 0)),
        pl.BlockSpec((tn, d), lambda i, j: (j, 0)),
    ]
    args = [x2d, g, wt]
    out_specs = [pl.BlockSpec((1, tn, tm), lambda i, j: (i // sb, j, i % sb))]
    out_shape = [jax.ShapeDtypeStruct((bsz, n, s), BF16)]
    blocks = tm * d * 4 + d * 4 + tn * d * 2 + tn * tm * 2
    if mode == "fox":
        body = functools.partial(_norm_proj_fox_kernel, q_blocks=q_blocks)
        in_specs.append(pl.BlockSpec((n_forget, d), lambda i, j: (n // n_forget, 0)))
        args.append(wt)
        out_specs.append(pl.BlockSpec((1, n_forget, tm), lambda i, j: (i // sb, 0, i % sb)))
        out_shape.append(jax.ShapeDtypeStruct((bsz, n_forget, s), F32))
        blocks += n_forget * d * 2 + n_forget * tm * 4
    else:
        body = _norm_proj_swa_kernel
        scale, rope, cos, sin = tables
        in_specs = [pl.BlockSpec(memory_space=pltpu.SMEM)] * 2 + in_specs
        in_specs += [pl.BlockSpec((ROT_HALF, tm), lambda i, j: (0, i % sb))] * 2
        args = [scale, rope] + args + [cos, sin]
        blocks += 2 * ROT_HALF * tm * 4
    return pl.pallas_call(
        body,
        grid=(m // tm, n // tn),
        in_specs=in_specs,
        out_specs=out_specs,
        out_shape=out_shape,
        scratch_shapes=[pltpu.VMEM((tm, d), BF16)],
        compiler_params=pltpu.CompilerParams(
            dimension_semantics=("parallel", "arbitrary"),
            vmem_limit_bytes=_vmem_limit(blocks, tm * d * 2, 3 * tm * d * 4 + 2 * tn * tm * 4),
        ),
        name="norm_proj_" + mode,
    )(*args)


def _decay_rows_kernel(z_ref, b_ref, qx_ref, kx_ref):
    z = z_ref[0] + b_ref[...]
    x = (jnp.minimum(z, 0.0) - jnp.log1p(jnp.exp(-jnp.abs(z)))) * LOG2E
    n_heads, s = x.shape
    col = lax.broadcasted_iota(jnp.int32, x.shape, 1)
    shift = 1
    while shift < s:
        x = x + jnp.where(col >= shift, pltpu.roll(x, shift, axis=1), 0.0)
        shift *= 2
    hi = x.astype(BF16).astype(F32)
    mid = (x - hi).astype(BF16).astype(F32)
    lo = (x - hi - mid).astype(BF16).astype(F32)
    row = lax.broadcasted_iota(jnp.int32, (XROWS, s), 0)
    for h in range(n_heads):
        parts_q = jnp.where(row == 0, hi[h:h + 1], jnp.where(row == 1, mid[h:h + 1], lo[h:h + 1]))
        parts_k = jnp.where(row == 3, hi[h:h + 1], jnp.where(row == 4, mid[h:h + 1], lo[h:h + 1]))
        qx = jnp.where(row < 3, parts_q, jnp.where(row < 6, 1.0, 0.0))
        kx = jnp.where(row < 3, 1.0, jnp.where(row < 6, -parts_k, 0.0))
        qx_ref[0, h] = qx.astype(BF16)
        kx_ref[0, h] = kx.astype(BF16)


def _decay_rows(z, b, n_heads):
    bsz, _, s = z.shape
    out_blk = n_heads * XROWS * s * 2
    return pl.pallas_call(
        _decay_rows_kernel,
        grid=(bsz,),
        in_specs=[
            pl.BlockSpec((1, n_heads, s), lambda i: (i, 0, 0)),
            pl.BlockSpec((n_heads, 1), lambda i: (0, 0)),
        ],
        out_specs=[pl.BlockSpec((1, n_heads, XROWS, s), lambda i: (i, 0, 0, 0))] * 2,
        out_shape=[jax.ShapeDtypeStruct((bsz, n_heads, XROWS, s), BF16)] * 2,
        compiler_params=pltpu.CompilerParams(
            dimension_semantics=("parallel",),
            vmem_limit_bytes=_vmem_limit(n_heads * s * 4 + 2 * out_blk, 0, 8 * n_heads * s * 4),
        ),
        name="decay_rows",
    )(z, b)


def _fox_attn_kernel(q_ref, k_ref, v_ref, gate_ref, qx_ref, kx_ref, o_ref,
                     qa_ref, s0_ref, s1_ref, bm0_ref, bm1_ref, m_ref, acc_ref, *, g_heads, tq, tk, n_sub):
    tiles = []
    for sub in range(n_sub):
        cols = slice(sub * tq, (sub + 1) * tq)
        tiles.append(_fox_query_tile(pl.program_id(2) * n_sub + sub,
                                     q_ref.at[:, :, cols], k_ref, v_ref, gate_ref.at[:, :, cols],
                                     qx_ref.at[:, :, :, cols], kx_ref, o_ref.at[:, :, cols],
                                     qa_ref.at[sub], s0_ref, s1_ref, bm0_ref, bm1_ref, m_ref.at[sub],
                                     acc_ref.at[sub], g_heads=g_heads, tq=tq, tk=tk))
    for init, _, _ in tiles:
        init()
    for g in range(g_heads):
        tiles[0][1](g)
    for sub, (_, _, run) in enumerate(tiles):
        run(tiles[sub + 1][1] if sub + 1 < n_sub else None)


def _fox_query_tile(qi, q_ref, k_ref, v_ref, gate_ref, qx_ref, kx_ref, o_ref,
                    qa_ref, s0_ref, s1_ref, bm0_ref, bm1_ref, m_ref, acc_ref, *, g_heads, tq, tk):
    n_diag = tq // tk
    n_full = qi * n_diag
    d = HEAD_DIM
    s_refs = (s0_ref, s1_ref)
    bm_refs = (bm0_ref, bm1_ref)

    def init():
        for g in range(g_heads):
            qa_ref[g, 0:d, :] = q_ref[0, g * d:(g + 1) * d, :]
            qa_ref[g, d:d + XROWS, :] = qx_ref[0, g]
        m_ref[...] = jnp.full(m_ref.shape, NEG_INF, F32)
        acc_ref[...] = jnp.zeros(acc_ref.shape, F32)

    ones_rows = (lax.broadcasted_iota(jnp.int32, (XROWS, tk), 0) == 0).astype(BF16)

    def scores(j, slot, g, col0=0):
        off = pl.multiple_of(j * tk, tk)
        ka = jnp.concatenate([k_ref[0, g * d:(g + 1) * d, pl.ds(off, tk)],
                              kx_ref[0, g, :, pl.ds(off, tk)]], axis=0)
        s = lax.dot_general(ka, qa_ref[g, :, col0:], (((0,), (0,)), ((), ())),
                            preferred_element_type=F32)
        s_refs[slot][g, :, col0:] = s
        if col0 == 0:
            bm_refs[slot][g] = jnp.max(s, axis=0, keepdims=True)

    def softmax_pv(j, slot, g, diag):
        col0 = 0 if diag is None else diag * tk
        off = pl.multiple_of(j * tk, tk)
        s = s_refs[slot][g, :, col0:]
        m_prev = m_ref[g, :, col0:]
        if diag is None:
            m_new = jnp.maximum(m_prev, bm_refs[slot][g])
            p = jnp.exp2(s - m_new).astype(BF16)
        else:
            hb = tk // 2
            keep = (lax.broadcasted_iota(jnp.int32, (hb, hb), 0) <= lax.broadcasted_iota(jnp.int32, (hb, hb), 1))
            tl = jnp.where(keep, s[0:hb, 0:hb], NEG_INF)
            tr = s[0:hb, hb:tk]
            br = jnp.where(keep, s[hb:tk, hb:tk], NEG_INF)
            bm = [jnp.max(tl, axis=0, keepdims=True),
                  jnp.maximum(jnp.max(tr, axis=0, keepdims=True), jnp.max(br, axis=0, keepdims=True))]
            if s.shape[1] > tk:
                bm.append(bm_refs[slot][g, :, col0 + tk:])
            m_new = jnp.maximum(m_prev, jnp.concatenate(bm, axis=1))
            m_l, m_r = m_new[:, 0:hb], m_new[:, hb:tk]
            top = jnp.concatenate([jnp.exp2(tl - m_l), jnp.exp2(tr - m_r)], axis=1).astype(BF16)
            bottom = jnp.concatenate([jnp.zeros((hb, hb), BF16), jnp.exp2(br - m_r).astype(BF16)], axis=1)
            p = jnp.concatenate([top, bottom], axis=0)
            if s.shape[1] > tk:
                p = jnp.concatenate([p, jnp.exp2(s[:, tk:] - m_new[:, tk:]).astype(BF16)], axis=1)
        alpha = jnp.exp2(m_prev - m_new)
        va = jnp.concatenate([v_ref[0, g * d:(g + 1) * d, pl.ds(off, tk)], ones_rows], axis=0)
        acc_ref[g, :, col0:] = alpha * acc_ref[g, :, col0:] + jnp.dot(va, p, preferred_element_type=F32)
        m_ref[g, :, col0:] = m_new

    def step(j, slot, diag, next_col0=0):
        for g in range(g_heads):
            if next_col0 is not None:
                scores(j + 1, 1 - slot, g, next_col0)
            softmax_pv(j, slot, g, diag)

    def first_scores(g):
        scores(0, 0, g)

    def steps(j0, count):
        for u in range(count):
            step(j0 + u, u % 2, None)

    def run(next_first_scores):
        assert n_diag == 2
        rem = n_full % 4

        @pl.when(rem == 2)
        def _():
            steps(0, 2)

        def four_steps(jj, carry):
            steps(rem + 4 * jj, 4)
            return carry

        lax.fori_loop(0, n_full // 4, four_steps, 0)
        step(n_full, 0, 0, next_col0=tk)
        for g in range(g_heads):
            if next_first_scores is not None:
                next_first_scores(g)
            softmax_pv(n_full + 1, 1, g, 1)

        for g in range(g_heads):
            y = acc_ref[g, 0:d, :] * (1.0 / acc_ref[g, d:d + 1, :])
            o_ref[0, g * d:(g + 1) * d, :] = _gated(y, gate_ref[0, g * d:(g + 1) * d, :])

    return init, first_scores, run


def _fox_attention(pt, qx, kx, *, n_heads, g_heads, tq, tk, n_sub):
    bsz, _, s = pt.shape
    d = HEAD_DIM
    rows = g_heads * d
    nw = n_heads // g_heads
    tqs = n_sub * tq
    blocks = 3 * rows * tqs * 2 + 2 * rows * s * 2 + g_heads * XROWS * (tqs + s) * 2
    scratch = g_heads * (n_sub * ((d + XROWS) * tq * 2 + 8 * tq * 4 + (d + XROWS) * tq * 4)
                         + 2 * 8 * tq * 4 + 2 * tk * tq * 4)
    return pl.pallas_call(
        functools.partial(_fox_attn_kernel, g_heads=g_heads, tq=tq, tk=tk, n_sub=n_sub),
        grid=(bsz, nw, s // tqs),
        in_specs=[
            pl.BlockSpec((1, rows, tqs), lambda b, hg, qi: (b, hg, qi)),
            pl.BlockSpec((1, rows, s), lambda b, hg, qi: (b, nw + hg, 0)),
            pl.BlockSpec((1, rows, s), lambda b, hg, qi: (b, 2 * nw + hg, 0)),
            pl.BlockSpec((1, rows, tqs), lambda b, hg, qi: (b, 3 * nw + hg, qi)),
            pl.BlockSpec((1, g_heads, XROWS, tqs), lambda b, hg, qi: (b, hg, 0, qi)),
            pl.BlockSpec((1, g_heads, XROWS, s), lambda b, hg, qi: (b, hg, 0, 0)),
        ],
        out_specs=pl.BlockSpec((1, rows, tqs), lambda b, hg, qi: (b, hg, qi)),
        out_shape=jax.ShapeDtypeStruct((bsz, n_heads * d, s), BF16),
        scratch_shapes=[
            pltpu.VMEM((n_sub, g_heads, d + XROWS, tq), BF16),
            pltpu.VMEM((g_heads, tk, tq), F32),
            pltpu.VMEM((g_heads, tk, tq), F32),
            pltpu.VMEM((g_heads, 1, tq), F32),
            pltpu.VMEM((g_heads, 1, tq), F32),
            pltpu.VMEM((n_sub, g_heads, 1, tq), F32),
            pltpu.VMEM((n_sub, g_heads, d + XROWS, tq), F32),
        ],
        compiler_params=pltpu.CompilerParams(
            dimension_semantics=("parallel", "parallel", "arbitrary"),
            vmem_limit_bytes=_vmem_limit(blocks, scratch, 4 * g_heads * tq * tk * 4),
        ),
        name="fox_attention",
    )(pt, pt, pt, pt, qx, kx)


def _swa_attn_kernel(q_ref, k_ref, v_ref, gate_ref, sink_ref, o_ref, *, blk, nsub):
    n = pl.program_id(2)
    d = HEAD_DIM
    g = q_ref.shape[1] // d
    assert blk == SWA_WINDOW
    ones_rows = (lax.broadcasted_iota(jnp.int32, (XROWS, 2 * blk), 0) == 0).astype(BF16)
    hw = g // 2
    j_loc = lax.broadcasted_iota(jnp.int32, (blk, hw * blk), 0)
    t_loc = lax.broadcasted_iota(jnp.int32, (blk, hw * blk), 1) & (blk - 1)
    from_prev = j_loc > t_loc

    def window_start(i):
        return pl.multiple_of(jnp.maximum(n * nsub + i - 1, 0) * blk, blk)

    tiles = [(i, h0) for i in range(nsub) for h0 in range(0, g, hw)]

    scores = []
    for i, h0 in tiles:
        qa = jnp.concatenate([q_ref[0, h * d:(h + 1) * d, i * blk:(i + 1) * blk] for h in range(h0, h0 + hw)],
                             axis=1)
        scores.append(lax.dot_general(k_ref[0, :, pl.ds(window_start(i), 2 * blk)], qa,
                                      (((0,), (0,)), ((), ())), preferred_element_type=F32))

    for (i, h0), sc in zip(tiles, scores):
        start = window_start(i)
        sink = sink_ref[0, :, h0 * blk:(h0 + hw) * blk]
        s_lo, s_hi = sc[0:blk], sc[blk:2 * blk]
        if i == 0:
            s_lo, s_hi = jnp.where(n == 0, NEG_INF, s_lo), jnp.where(n == 0, s_lo, s_hi)
        s = jnp.where(from_prev, s_lo, s_hi)
        m = jnp.maximum(jnp.max(s, axis=0, keepdims=True), sink)
        e = jnp.exp2(s - m).astype(BF16)
        zero = jnp.zeros_like(e)
        e_win = jnp.concatenate([jnp.where(from_prev, e, zero), jnp.where(from_prev, zero, e)], axis=0)
        if i == 0:
            e_first = jnp.concatenate([jnp.where(from_prev, zero, e), zero], axis=0)
            e_win = jnp.where(n == 0, e_first, e_win)
        va = jnp.concatenate([v_ref[0, :, pl.ds(start, 2 * blk)], ones_rows], axis=0)
        o = jnp.dot(va, e_win, preferred_element_type=F32)
        denom = o[d:d + 1] + jnp.exp2(sink - m)
        y = o[0:d] * (1.0 / denom)
        for hh in range(hw):
            h = h0 + hh
            gate = gate_ref[0, h * d:(h + 1) * d, i * blk:(i + 1) * blk]
            o_ref[0, h * d:(h + 1) * d, i * blk:(i + 1) * blk] = _gated(y[:, hh * blk:(hh + 1) * blk], gate)


def _swa_attention(pt, sink_rows, *, blk, nsub):
    bsz, _, s = pt.shape
    d = HEAD_DIM
    rows = SWA_GROUP * d
    wq, wk = SWA_Q_HEADS * d, SWA_KV_HEADS * d
    tq = nsub * blk
    k0, v0, g0 = wq // d, (wq + wk) // d, (wq + 2 * wk) // rows
    blocks = 3 * rows * tq * 2 + 2 * d * s * 2 + 8 * SWA_GROUP * blk * 4
    return pl.pallas_call(
        functools.partial(_swa_attn_kernel, blk=blk, nsub=nsub),
        grid=(bsz, SWA_KV_HEADS, s // tq),
        in_specs=[
            pl.BlockSpec((1, rows, tq), lambda b, h, n: (b, h, n)),
            pl.BlockSpec((1, d, s), lambda b, h, n: (b, k0 + h, 0)),
            pl.BlockSpec((1, d, s), lambda b, h, n: (b, v0 + h, 0)),
            pl.BlockSpec((1, rows, tq), lambda b, h, n: (b, g0 + h, n)),
            pl.BlockSpec((1, 1, SWA_GROUP * blk), lambda b, h, n: (h, 0, 0)),
        ],
        out_specs=pl.BlockSpec((1, rows, tq), lambda b, h, n: (b, h, n)),
        out_shape=jax.ShapeDtypeStruct((bsz, wq, s), BF16),
        compiler_params=pltpu.CompilerParams(
            dimension_semantics=("parallel", "parallel", "arbitrary"),
            vmem_limit_bytes=_vmem_limit(blocks, 0, 8 * nsub * 2 * blk * SWA_GROUP * blk * 4),
        ),
        name="swa_attention",
    )(pt, pt, pt, pt, sink_rows)


def _out_proj_kernel(zt_ref, x_ref, w_ref, fg_ref, o_ref, *, final_norm):
    out = x_ref[...] + lax.dot_general(zt_ref[0], w_ref[...], (((0,), (0,)), ((), ())),
                                       preferred_element_type=F32)
    if final_norm:
        ms = jnp.mean(out * out, axis=-1, keepdims=True)
        out = (out * lax.rsqrt(ms + RMS_EPS)) * fg_ref[...]
    o_ref[...] = out


def _out_proj(zt, x2d, w, fg, *, tm, final_norm):
    bsz, wid, s = zt.shape
    m, d = x2d.shape
    sb = s // tm
    blocks = wid * tm * 2 + 2 * tm * d * 4 + wid * d * 2 + d * 4
    return pl.pallas_call(
        functools.partial(_out_proj_kernel, final_norm=final_norm),
        grid=(m // tm,),
        in_specs=[
            pl.BlockSpec((1, wid, tm), lambda i: (i // sb, 0, i % sb)),
            pl.BlockSpec((tm, d), lambda i: (i, 0)),
            pl.BlockSpec((wid, d), lambda i: (0, 0), pipeline_mode=pl.Buffered(1)),
            pl.BlockSpec((1, d), lambda i: (0, 0)),
        ],
        out_specs=pl.BlockSpec((tm, d), lambda i: (i, 0)),
        out_shape=jax.ShapeDtypeStruct((m, d), F32),
        compiler_params=pltpu.CompilerParams(
            dimension_semantics=("parallel",),
            vmem_limit_bytes=_vmem_limit(blocks, 0, wid * tm * 2 + 3 * tm * d * 4),
        ),
        name="out_proj",
    )(zt, x2d, w, fg)


def _rope_tables(s):
    inv_freq = ROPE_THETA ** (-jnp.arange(ROT_HALF, dtype=F32) / ROT_HALF)
    ang = jnp.arange(s, dtype=F32)[:, None] * inv_freq[None, :]
    return jnp.cos(ang).T, jnp.sin(ang).T


PROJ_TM = 1024
FOX_PROJ_TN = 2048
SWA_PROJ_TN = 1536
WT_TN = 512
OUT_TM = 1024
FOX_G, FOX_TQ, FOX_TK = 8, 512, 256
FOX_NSUB = 4
SWA_NSUB = 32


def kernel(x, norm_g, fox_w_in, fox_b_f, fox_w_out, swa_w_in, swa_sinks, swa_w_out, final_g):
    bsz, s, d = x.shape
    x2d = x.reshape(bsz * s, d)

    wid = FOX_HEADS * HEAD_DIM
    wt = fox_w_in[0].T.astype(BF16)
    pt, zt = _norm_proj(x2d, norm_g[0][None, :], wt, bsz=bsz, mode="fox", n_forget=FOX_HEADS,
                        tm=PROJ_TM, tn=FOX_PROJ_TN, q_blocks=wid // FOX_PROJ_TN)
    qx, kx = _decay_rows(zt, fox_b_f[0][:, None], FOX_HEADS)
    yt = _fox_attention(pt, qx, kx, n_heads=FOX_HEADS, g_heads=FOX_G, tq=FOX_TQ, tk=FOX_TK, n_sub=FOX_NSUB)
    x2d = _out_proj(yt, x2d, fox_w_out[0].astype(BF16), final_g[None, :], tm=OUT_TM, final_norm=False)

    wq = SWA_Q_HEADS * HEAD_DIM
    n_swa = swa_w_in.shape[2]
    wt = _transpose_cast(swa_w_in[0], n_swa, WT_TN)
    head = jnp.arange(n_swa // HEAD_DIM)
    head_scale = jnp.where(head < SWA_Q_HEADS, QK_SCALE, 1.0).astype(F32)
    head_rope = (head < SWA_Q_HEADS + SWA_KV_HEADS).astype(F32)
    (pt,) = _norm_proj(x2d, norm_g[1][None, :], wt, bsz=bsz, mode="swa", tm=PROJ_TM, tn=SWA_PROJ_TN,
                       tables=(head_scale, head_rope) + _rope_tables(s))
    sink_rows = jnp.repeat(swa_sinks[0].reshape(SWA_KV_HEADS, SWA_GROUP) * LOG2E, SWA_WINDOW, axis=1)[:, None, :]
    yt = _swa_attention(pt, sink_rows, blk=SWA_WINDOW, nsub=SWA_NSUB)
    out = _out_proj(yt, x2d, swa_w_out[0].astype(BF16), final_g[None, :], tm=OUT_TM, final_norm=True)
    return out.reshape(bsz, s, d)
```
